```python
import math, functools
import jax, jax.numpy as jnp
from jax import lax
import numpy as np

D_MODEL = 2048
BATCH = 4
SEQ = 2048
DEPTH = 1
DEC_BATCH = 128
DEC_SEQ = 8
PAST_LEN = 2048
PAGE_SIZE = 128

N_HEADS = 8
D_HEAD = 128
D_VHEAD = 2 * D_HEAD
QK_W = N_HEADS * 2 * D_HEAD
ATTN_W = N_HEADS * D_VHEAD
QBLOCK = 128
SCALE = D_HEAD ** -0.5
N_BUCKETS = 32
MAX_DISTANCE = 128
CHUNK = 128
GMLP_GROUPS = 16
GMLP_W = D_MODEL
GMLP_GDIM = GMLP_W // GMLP_GROUPS
D_FF = 5632
EPS = 1e-6
IN_WIDTHS = (QK_W, QK_W, ATTN_W, GMLP_W, GMLP_W, ATTN_W, GMLP_W)
IN_W = sum(IN_WIDTHS)
IN_SPLITS = [int(s) for s in np.cumsum(IN_WIDTHS)[:-1]]

kernel_name = "hybrid_diffattn_gmlp_macaron_step"


def _rms(x, g):
    xf = x.astype(jnp.float32)
    y = xf * lax.rsqrt(jnp.mean(xf * xf, axis=-1, keepdims=True) + EPS)
    return (y * g.astype(jnp.float32)).astype(x.dtype)


def _layernorm(x, g, b):
    xf = x.astype(jnp.float32)
    mu = jnp.mean(xf, axis=-1, keepdims=True)
    xc = xf - mu
    y = xc * lax.rsqrt(jnp.mean(xc * xc, axis=-1, keepdims=True) + EPS)
    return (y * g.astype(jnp.float32) + b.astype(jnp.float32)).astype(x.dtype)


def _swiglu(x, w_gate, w_up, w_down):
    return (jax.nn.silu(x @ w_gate) * (x @ w_up)) @ w_down


def _rel_bucket(q_pos, k_pos):
    n = jnp.maximum(q_pos[:, None] - k_pos[None, :], 0)
    max_exact = N_BUCKETS // 2
    nf = jnp.maximum(n, 1).astype(jnp.float32)
    large = max_exact + (jnp.log(nf / max_exact) / math.log(MAX_DISTANCE / max_exact)
                         * (N_BUCKETS - max_exact)).astype(jnp.int32)
    large = jnp.minimum(large, N_BUCKETS - 1)
    return jnp.where(n < max_exact, n, large)


def _pos_bias(rel_bias, q_pos, k_pos):
    return jnp.transpose(rel_bias[_rel_bucket(q_pos, k_pos)], (2, 0, 1)).astype(jnp.float32)


def _diff_attend(q, k, v, bias, mask, lam):
    s = jnp.einsum('bqhmd,bkhmd->bhmqk', q, k, preferred_element_type=jnp.float32) * SCALE
    s = jnp.where(mask, s + bias[None, :, None], -jnp.inf)
    p = jax.nn.softmax(s, axis=-1)
    a = p[:, :, 0] - lam * p[:, :, 1]
    return jnp.einsum('bhqk,bkhe->bqhe', a.astype(v.dtype), v)


def _attend_prompt(q, k, v, lam, rel_bias):
    B, S = q.shape[0], q.shape[1]
    nb = S // QBLOCK
    k_pos = jnp.arange(S)
    qb = jnp.moveaxis(q.reshape(B, nb, QBLOCK, N_HEADS, 2, D_HEAD), 1, 0)

    def one_block(args):
        i, qi = args
        q_pos = i * QBLOCK + jnp.arange(QBLOCK)
        return _diff_attend(qi, k, v, _pos_bias(rel_bias, q_pos, k_pos),
                            q_pos[:, None] >= k_pos[None, :], lam)

    o = lax.map(one_block, (jnp.arange(nb), qb))
    return jnp.moveaxis(o, 0, 1).reshape(B, S, N_HEADS, D_VHEAD)


def _attend_sample(q, k, v, lam, rel_bias, cache_k, cache_v, page_table, layer):
    T = q.shape[1]
    past = page_table.shape[1] * PAGE_SIZE
    q_pos = past + jnp.arange(T)
    k_pos = jnp.arange(past + T)
    bias = _pos_bias(rel_bias, q_pos, k_pos)
    mask = q_pos[:, None] >= k_pos[None, :]

    def one_seq(args):
        pt, qs, ks, vs = args
        kp = cache_k[layer, pt].reshape(past, N_HEADS, 2, D_HEAD)
        vp = cache_v[layer, pt].reshape(past, N_HEADS, D_VHEAD)
        kk = jnp.concatenate([kp, ks.astype(kp.dtype)], axis=0)[None]
        vv = jnp.concatenate([vp, vs.astype(vp.dtype)], axis=0)[None]
        return _diff_attend(qs[None], kk, vv, bias, mask, lam)[0]

    return lax.map(one_seq, (page_table, q, k, v))


def _spatial_mix(sv, w_s, b_s, chunk_len):
    B, T, _ = sv.shape
    c = T // chunk_len
    vv = sv.reshape(B, c, chunk_len, GMLP_GROUPS, GMLP_GDIM)
    w = jnp.tril(w_s[:, :chunk_len, :chunk_len])
    mixed = jnp.einsum('gts,bcsgd->bctgd', w, vv) + b_s[:, :chunk_len].T[None, None, :, :, None]
    return mixed.reshape(B, T, GMLP_W)


def _layer(x, attend, chunk_len, lam_init, p):
    B, T, _ = x.shape
    h = x + 0.5 * _rms(_swiglu(_rms(x, p['ffn1_norm_pre']), p['ffn1_w_gate'], p['ffn1_w_up'],
                               p['ffn1_w_down']), p['ffn1_norm_post'])
    n = _rms(h, p['mix_norm_pre'])
    z = n @ p['w_in']
    q, k, v, gu, gv, ga, gm = jnp.split(z, IN_SPLITS, axis=-1)
    q = q.reshape(B, T, N_HEADS, 2, D_HEAD)
    k = k.reshape(B, T, N_HEADS, 2, D_HEAD)
    v = v.reshape(B, T, N_HEADS, D_VHEAD)
    f32 = lambda a: a.astype(jnp.float32)
    lam = (jnp.exp(jnp.sum(f32(p['lambda_q1']) * f32(p['lambda_k1'])))
           - jnp.exp(jnp.sum(f32(p['lambda_q2']) * f32(p['lambda_k2']))) + lam_init)
    o = attend(q, k, v, lam)
    o_attn = (_rms(o, p['attn_head_norm']) * (1.0 - lam_init)).reshape(B, T, ATTN_W)
    u = jax.nn.gelu(gu, approximate=False)
    sv = _layernorm(jax.nn.gelu(gv, approximate=False), p['gmlp_ln_g'], p['gmlp_ln_b'])
    o_gmlp = u * _spatial_mix(sv, p['gmlp_w_s'], p['gmlp_b_s'], chunk_len)
    merged = jax.nn.sigmoid(ga) * o_attn + jax.nn.sigmoid(gm) * o_gmlp
    h = h + _rms(merged @ p['w_o'], p['mix_norm_post'])
    y = h + 0.5 * _rms(_swiglu(_rms(h, p['ffn2_norm_pre']), p['ffn2_w_gate'], p['ffn2_w_up'],
                               p['ffn2_w_down']), p['ffn2_norm_post'])
    return y, k.reshape(B, T, N_HEADS, 2 * D_HEAD), v, sv


def setup_inputs(seed: int = 0) -> dict:
    key = jax.random.key(seed)
    keys = iter(jax.random.split(key, 40))
    nrm = lambda shape, scale: jax.random.normal(next(keys), shape, jnp.float32) * scale
    gain = lambda shape: 1.0 + nrm(shape, 0.05)
    n_pages = PAST_LEN // PAGE_SIZE
    n_used = DEC_BATCH * n_pages
    n_phys = n_used + max(1, n_used // 4)
    perm = jax.random.permutation(next(keys), n_phys)
    page_table = perm[:n_used].reshape(DEC_BATCH, n_pages).astype(jnp.int32)
    L = DEPTH
    return {
        'x_prompt': nrm((BATCH, SEQ, D_MODEL), 1.0),
        'x_sample': nrm((DEC_BATCH, DEC_SEQ, D_MODEL), 1.0),
        'cache_k': nrm((L, n_phys, PAGE_SIZE, N_HEADS, 2 * D_HEAD), 1.0),
        'cache_v': nrm((L, n_phys, PAGE_SIZE, N_HEADS, D_VHEAD), 1.0),
        'page_table': page_table,
        'rel_bias': nrm((N_BUCKETS, N_HEADS), 0.5),
        'ffn1_norm_pre': gain((L, D_MODEL)),
        'ffn1_w_gate': nrm((L, D_MODEL, D_FF), D_MODEL ** -0.5),
        'ffn1_w_up': nrm((L, D_MODEL, D_FF), D_MODEL ** -0.5),
        'ffn1_w_down': nrm((L, D_FF, D_MODEL), D_FF ** -0.5),
        'ffn1_norm_post': gain((L, D_MODEL)),
        'mix_norm_pre': gain((L, D_MODEL)),
        'w_in': nrm((L, D_MODEL, IN_W), D_MODEL ** -0.5),
        'lambda_q1': nrm((L, D_HEAD), 0.1),
        'lambda_k1': nrm((L, D_HEAD), 0.1),
        'lambda_q2': nrm((L, D_HEAD), 0.1),
        'lambda_k2': nrm((L, D_HEAD), 0.1),
        'attn_head_norm': gain((L, D_VHEAD)),
        'gmlp_ln_g': gain((L, GMLP_W)),
        'gmlp_ln_b': nrm((L, GMLP_W), 0.02),
        'gmlp_w_s': nrm((L, GMLP_GROUPS, CHUNK, CHUNK), CHUNK ** -0.5),
        'gmlp_b_s': 1.0 + nrm((L, GMLP_GROUPS, CHUNK), 0.1),
        'w_o': nrm((L, ATTN_W, D_MODEL), ATTN_W ** -0.5),
        'mix_norm_post': gain((L, D_MODEL)),
        'ffn2_norm_pre': gain((L, D_MODEL)),
        'ffn2_w_gate': nrm((L, D_MODEL, D_FF), D_MODEL ** -0.5),
        'ffn2_w_up': nrm((L, D_MODEL, D_FF), D_MODEL ** -0.5),
        'ffn2_w_down': nrm((L, D_FF, D_MODEL), D_FF ** -0.5),
        'ffn2_norm_post': gain((L, D_MODEL)),
    }


def reference(x_prompt, x_sample, cache_k, cache_v, page_table, rel_bias,
              ffn1_norm_pre, ffn1_w_gate, ffn1_w_up, ffn1_w_down, ffn1_norm_post,
              mix_norm_pre, w_in, lambda_q1, lambda_k1, lambda_q2, lambda_k2, attn_head_norm,
              gmlp_ln_g, gmlp_ln_b, gmlp_w_s, gmlp_b_s, w_o, mix_norm_post,
              ffn2_norm_pre, ffn2_w_gate, ffn2_w_up, ffn2_w_down, ffn2_norm_post):
    yp, ys = x_prompt, x_sample
    kp_rows, vp_rows, ks_rows, vs_rows, gs_rows = [], [], [], [], []
    for l in range(DEPTH):
        lp = {
            'ffn1_norm_pre': ffn1_norm_pre[l], 'ffn1_w_gate': ffn1_w_gate[l], 'ffn1_w_up': ffn1_w_up[l],
            'ffn1_w_down': ffn1_w_down[l], 'ffn1_norm_post': ffn1_norm_post[l],
            'mix_norm_pre': mix_norm_pre[l], 'w_in': w_in[l],
            'lambda_q1': lambda_q1[l], 'lambda_k1': lambda_k1[l],
            'lambda_q2': lambda_q2[l], 'lambda_k2': lambda_k2[l],
            'attn_head_norm': attn_head_norm[l], 'gmlp_ln_g': gmlp_ln_g[l], 'gmlp_ln_b': gmlp_ln_b[l],
            'gmlp_w_s': gmlp_w_s[l], 'gmlp_b_s': gmlp_b_s[l], 'w_o': w_o[l],
            'mix_norm_post': mix_norm_post[l],
            'ffn2_norm_pre': ffn2_norm_pre[l], 'ffn2_w_gate': ffn2_w_gate[l], 'ffn2_w_up': ffn2_w_up[l],
            'ffn2_w_down': ffn2_w_down[l], 'ffn2_norm_post': ffn2_norm_post[l],
        }
        lam_init = 0.8 - 0.6 * math.exp(-0.3 * l)
        attend_p = functools.partial(_attend_prompt, rel_bias=rel_bias)
        attend_s = functools.partial(_attend_sample, rel_bias=rel_bias, cache_k=cache_k,
                                     cache_v=cache_v, page_table=page_table, layer=l)
        yp, k_p, v_p, _ = _layer(yp, attend_p, CHUNK, lam_init, lp)
        ys, k_s, v_s, g_s = _layer(ys, attend_s, ys.shape[1], lam_init, lp)
        kp_rows.append(k_p)
        vp_rows.append(v_p)
        ks_rows.append(k_s)
        vs_rows.append(v_s)
        gs_rows.append(g_s)
    k_prompt = jnp.stack(kp_rows, axis=0)
    v_prompt = jnp.stack(vp_rows, axis=0)
    k_sample = jnp.stack(ks_rows, axis=0)
    v_sample = jnp.stack(vs_rows, axis=0)
    gmlp_v_sample = jnp.stack(gs_rows, axis=0)
    return (yp, ys, k_prompt, v_prompt, k_sample, v_sample, gmlp_v_sample)
```

```python
import functools
import math

import jax
import jax.numpy as jnp
import numpy as np
from jax import lax
from jax.experimental import pallas as pl
from jax.experimental.pallas import tpu as pltpu

F32 = jnp.float32
BF16 = jnp.bfloat16

D_MODEL = 2048
D_FF = 5632
N_HEADS = 8
D_HEAD = 128
D_VHEAD = 256
SEC_W = 2048
N_BUCKETS = 32
MAX_DISTANCE = 128
PAGE = 128
CHUNK = 128
GROUPS = 16
GDIM = 128
EPS = 1e-6
SCALE = D_HEAD ** -0.5
LAM_INIT = 0.8 - 0.6 * math.exp(-0.3 * 0)

VMEM_LIMIT = 56 * 1024 * 1024

TM_FFN = 512
TF_FFN = 512
TM_PROJ = 512
T_ATT = 512


def _rms_rows(x, g):
    return x * lax.rsqrt(jnp.mean(x * x, axis=-1, keepdims=True) + EPS) * g


def _ffn_kernel(x_ref, gpre_ref, wg_ref, wu_ref, wd_ref, gpost_ref, gnext_ref,
                h_ref, *rest, emit_next):
    if emit_next:
        n_ref, xn_scr, acc_scr = rest
    else:
        xn_scr, acc_scr = rest
    j = pl.program_id(1)

    @pl.when(j == 0)
    def _():
        xn_scr[...] = _rms_rows(x_ref[...], gpre_ref[...]).astype(BF16)

    xn = xn_scr[...]
    g = jnp.dot(xn, wg_ref[...], preferred_element_type=F32)
    u = jnp.dot(xn, wu_ref[...], preferred_element_type=F32)
    a = (g * jax.nn.sigmoid(g) * u).astype(BF16)
    part = jnp.dot(a, wd_ref[...], preferred_element_type=F32)

    @pl.when(j == 0)
    def _():
        acc_scr[...] = part

    @pl.when(j > 0)
    def _():
        acc_scr[...] += part

    @pl.when(j == pl.num_programs(1) - 1)
    def _():
        h = x_ref[...] + 0.5 * _rms_rows(acc_scr[...], gpost_ref[...])
        h_ref[...] = h
        if emit_next:
            n_ref[...] = _rms_rows(h, gnext_ref[...]).astype(BF16)


def _ffn(x, gpre, wg, wu, wd, gpost, gnext, *, row0, rows, emit_next):
    nb = rows // TM_FFN
    b0 = row0 // TM_FFN
    nf = D_FF // TF_FFN
    vec = pl.BlockSpec((1, D_MODEL), lambda i, j: (0, 0))
    row_blk = pl.BlockSpec((TM_FFN, D_MODEL), lambda i, j: (i, 0))
    out_shape = [jax.ShapeDtypeStruct((rows, D_MODEL), F32)]
    out_specs = [row_blk]
    if emit_next:
        out_shape.append(jax.ShapeDtypeStruct((rows, D_MODEL), BF16))
        out_specs.append(row_blk)
    res = pl.pallas_call(
        functools.partial(_ffn_kernel, emit_next=emit_next),
        grid=(nb, nf),
        in_specs=[
            pl.BlockSpec((TM_FFN, D_MODEL), lambda i, j: (i + b0, 0)),
            vec,
            pl.BlockSpec((D_MODEL, TF_FFN), lambda i, j: (0, j)),
            pl.BlockSpec((D_MODEL, TF_FFN), lambda i, j: (0, j)),
            pl.BlockSpec((TF_FFN, D_MODEL), lambda i, j: (j, 0)),
            vec,
            vec,
        ],
        out_specs=out_specs,
        out_shape=out_shape,
        scratch_shapes=[pltpu.VMEM((TM_FFN, D_MODEL), BF16),
                        pltpu.VMEM((TM_FFN, D_MODEL), F32)],
        compiler_params=pltpu.CompilerParams(
            dimension_semantics=("parallel", "arbitrary"),
            vmem_limit_bytes=VMEM_LIMIT),
        name="ffn_emit_next" if emit_next else "ffn",
    )(x, gpre, wg, wu, wd, gpost, gnext)
    return res


def _gelu(x):
    return 0.5 * x * (1.0 + lax.erf(x * (1.0 / math.sqrt(2.0))))


def _proj_kernel(n_ref, w_ref, *rest, mode):
    z = jnp.dot(n_ref[...], w_ref[...], preferred_element_type=F32)
    if mode == "q_bf16":
        (o_ref,) = rest
        o_ref[...] = (z * SCALE).astype(BF16)
    elif mode == "q_f32":
        (o_ref,) = rest
        o_ref[...] = z * SCALE
    elif mode == "kv":
        o_ref, ob_ref = rest
        for h in range(N_HEADS):
            o_ref[:, h, :] = z[:, h * D_VHEAD:(h + 1) * D_VHEAD]
        ob_ref[...] = z.astype(BF16)
    elif mode == "gelu":
        (o_ref,) = rest
        o_ref[...] = _gelu(z)
    elif mode == "gelu_ln":
        g_ref, b_ref, o_ref = rest
        a = _gelu(z)
        mu = jnp.mean(a, axis=-1, keepdims=True)
        ac = a - mu
        y = ac * lax.rsqrt(jnp.mean(ac * ac, axis=-1, keepdims=True) + EPS)
        o_ref[...] = y * g_ref[...] + b_ref[...]
    elif mode == "sigmoid":
        (o_ref,) = rest
        o_ref[...] = jax.nn.sigmoid(z)
    else:
        raise ValueError(mode)


def _proj(n, w_in, *, sec0, nsec, row0, rows, mode, extra=()):
    nb = rows // TM_PROJ
    b0 = row0 // TM_PROJ
    blk = pl.BlockSpec((TM_PROJ, SEC_W), lambda s, i: (i, s))
    if mode == "q_bf16":
        out_dtypes = [BF16]
    elif mode == "kv":
        out_dtypes = [F32, BF16]
    else:
        out_dtypes = [F32]
    out_shape = [jax.ShapeDtypeStruct((rows, nsec * SEC_W), dt) for dt in out_dtypes]
    out_specs = [blk] * len(out_dtypes)
    if mode == "kv":
        assert nsec == 1
        out_shape[0] = jax.ShapeDtypeStruct((rows, N_HEADS, D_VHEAD), F32)
        out_specs[0] = pl.BlockSpec((TM_PROJ, N_HEADS, D_VHEAD), lambda s, i: (i, 0, 0))
    vec = pl.BlockSpec((1, SEC_W), lambda s, i: (0, 0))
    res = pl.pallas_call(
        functools.partial(_proj_kernel, mode=mode),
        grid=(nsec, nb),
        in_specs=[
            pl.BlockSpec((TM_PROJ, D_MODEL), lambda s, i: (i + b0, 0)),
            pl.BlockSpec((D_MODEL, SEC_W), lambda s, i: (0, s + sec0)),
        ] + [vec] * len(extra),
        out_specs=out_specs,
        out_shape=out_shape,
        compiler_params=pltpu.CompilerParams(
            dimension_semantics=("parallel", "parallel"),
            vmem_limit_bytes=VMEM_LIMIT),
        name="proj_" + mode,
    )(n, w_in, *extra)
    return res


def _lambda_full(lq1_ref, lk1_ref, lq2_ref, lk2_ref):
    s1 = jnp.sum(lq1_ref[...] * lk1_ref[...], axis=-1, keepdims=True)
    s2 = jnp.sum(lq2_ref[...] * lk2_ref[...], axis=-1, keepdims=True)
    return jnp.exp(s1) - jnp.exp(s2) + LAM_INIT


def _head_norm(o, gh):
    return _rms_rows(o, gh) * (1.0 - LAM_INIT)


def _bias_by_distance(rel_bias, nmax):
    n = jnp.arange(nmax)
    max_exact = N_BUCKETS // 2
    nf = jnp.maximum(n, 1).astype(F32)
    large = max_exact + (jnp.log(nf / max_exact) / math.log(MAX_DISTANCE / max_exact)
                         * (N_BUCKETS - max_exact)).astype(jnp.int32)
    large = jnp.minimum(large, N_BUCKETS - 1)
    bucket = jnp.where(n < max_exact, n, large)
    return rel_bias[bucket].astype(F32)


def _attn_prompt_kernel(qi_tab, ki_tab, q_ref, k_ref, v_ref, bias_ref,
                        lq1_ref, lk1_ref, lq2_ref, lk2_ref, gh_ref,
                        o_ref, m_scr, l_scr, acc_scr):
    t = pl.program_id(2)
    qi = qi_tab[t]
    ki = ki_tab[t]
    d = qi - ki

    @pl.when(ki == 0)
    def _():
        m_scr[...] = jnp.full(m_scr.shape, -jnp.inf, F32)
        l_scr[...] = jnp.zeros(l_scr.shape, F32)
        acc_scr[...] = jnp.zeros(acc_scr.shape, F32)

    q = q_ref[...]
    k = k_ref[...]
    v = v_ref[...]
    nt = (((1,), (1,)), ((), ()))

    def scores(mi):
        sl = slice(mi * D_HEAD, (mi + 1) * D_HEAD)
        return lax.dot_general(q[:, sl], k[:, sl], nt, preferred_element_type=F32)

    def update(mi, s):
        m_old = m_scr[mi]
        m_new = jnp.maximum(m_old, jnp.max(s, axis=-1, keepdims=True))
        alpha = jnp.exp(m_old - m_new)
        p = jnp.exp(s - m_new)
        l_scr[mi] = alpha * l_scr[mi] + jnp.sum(p, axis=-1, keepdims=True)
        acc_scr[mi] = alpha * acc_scr[mi] + jnp.dot(p.astype(BF16), v,
                                                    preferred_element_type=F32)
        m_scr[mi] = m_new

    @pl.when(d >= 2)
    def _():
        for mi in range(2):
            update(mi, scores(mi))

    @pl.when(d < 2)
    def _():
        rows = lax.broadcasted_iota(jnp.int32, (T_ATT, T_ATT), 0)
        cols = lax.broadcasted_iota(jnp.int32, (T_ATT, T_ATT), 1)
        keep = (rows >= cols) | (d == 1)
        b = bias_ref[0, 0]
        for mi in range(2):
            update(mi, jnp.where(keep, scores(mi) + b, -jnp.inf))

    @pl.when(d == 0)
    def _():
        lam = _lambda_full(lq1_ref, lk1_ref, lq2_ref, lk2_ref)
        o = acc_scr[0] / l_scr[0] - lam * (acc_scr[1] / l_scr[1])
        o_ref[...] = _head_norm(o, gh_ref[...])


def _attn_prompt(q, k, v, bias_tiles, lams, gh, *, batch, seq):
    nq = seq // T_ATT
    tri = [(qi, ki) for qi in range(nq) for ki in range(qi + 1)]
    qi_tab = jnp.asarray([a for a, _ in tri], jnp.int32)
    ki_tab = jnp.asarray([b for _, b in tri], jnp.int32)
    vec = pl.BlockSpec((1, D_HEAD), lambda b, h, t, qt, kt: (0, 0))
    grid_spec = pltpu.PrefetchScalarGridSpec(
        num_scalar_prefetch=2,
        grid=(batch, N_HEADS, len(tri)),
        in_specs=[
            pl.BlockSpec((T_ATT, 2 * D_HEAD), lambda b, h, t, qt, kt: (b * nq + qt[t], h)),
            pl.BlockSpec((T_ATT, 2 * D_HEAD), lambda b, h, t, qt, kt: (b * nq + kt[t], h)),
            pl.BlockSpec((T_ATT, D_VHEAD), lambda b, h, t, qt, kt: (b * nq + kt[t], h)),
            pl.BlockSpec((1, 1, T_ATT, T_ATT),
                         lambda b, h, t, qt, kt: (h, jnp.minimum(qt[t] - kt[t], 1), 0, 0)),
            vec, vec, vec, vec,
            pl.BlockSpec((1, D_VHEAD), lambda b, h, t, qt, kt: (0, 0)),
        ],
        out_specs=pl.BlockSpec((T_ATT, D_VHEAD), lambda b, h, t, qt, kt: (b * nq + qt[t], h)),
        scratch_shapes=[pltpu.VMEM((2, T_ATT, 1), F32),
                        pltpu.VMEM((2, T_ATT, 1), F32),
                        pltpu.VMEM((2, T_ATT, D_VHEAD), F32)],
    )
    return pl.pallas_call(
        _attn_prompt_kernel,
        grid_spec=grid_spec,
        out_shape=jax.ShapeDtypeStruct((batch * seq, N_HEADS * D_VHEAD), F32),
        compiler_params=pltpu.CompilerParams(
            dimension_semantics=("parallel", "parallel", "arbitrary"),
            vmem_limit_bytes=VMEM_LIMIT),
        name="attn_prompt",
    )(qi_tab, ki_tab, q, k, v, bias_tiles, *lams, gh)


ROWS_S = 128


def _attn_sample_kernel(pt_ref, q_ref, kc_ref, vc_ref, kn_ref, vn_ref, bias_ref,
                        lq1_ref, lk1_ref, lq2_ref, lk2_ref, gh_ref,
                        o_ref, qx_scr, m_scr, l_scr, acc_scr, *, n_pages, t_new):
    p = pl.program_id(1)

    @pl.when(p == 0)
    def _():
        q = q_ref[0]
        qt = jnp.broadcast_to(q[None], (ROWS_S // t_new, t_new, 2 * N_HEADS * D_HEAD))
        qt = qt.reshape(ROWS_S, 2 * N_HEADS * D_HEAD)
        rows = lax.broadcasted_iota(jnp.int32, qt.shape, 0)
        cols = lax.broadcasted_iota(jnp.int32, qt.shape, 1)
        qx_scr[...] = jnp.where(rows // t_new == cols // D_HEAD, qt, 0.0).astype(BF16)
        m_scr[...] = jnp.full(m_scr.shape, -jnp.inf, F32)
        l_scr[...] = jnp.zeros(l_scr.shape, F32)
        acc_scr[...] = jnp.zeros(acc_scr.shape, F32)

    def page(kb, vb):
        s = lax.dot_general(qx_scr[...], kb, (((1,), (1,)), ((), ())),
                            preferred_element_type=F32)
        s = s + bias_ref[0]
        m_old = m_scr[...]
        m_new = jnp.maximum(m_old, jnp.max(s, axis=-1, keepdims=True))
        alpha = jnp.exp(m_old - m_new)
        pr = jnp.exp(s - m_new)
        l_scr[...] = alpha * l_scr[...] + jnp.sum(pr, axis=-1, keepdims=True)
        m_scr[...] = m_new
        prb = pr.astype(BF16)
        rph = 2 * t_new
        for h in range(N_HEADS):
            rs = slice(h * rph, (h + 1) * rph)
            pv = jnp.dot(prb[rs], vb[:, h * D_VHEAD:(h + 1) * D_VHEAD],
                         preferred_element_type=F32)
            acc_scr[h] = alpha[rs] * acc_scr[h] + pv

    def by_position(ref):
        return jnp.concatenate([ref[:, h, :] for h in range(N_HEADS)], axis=1)

    @pl.when(p < n_pages)
    def _():
        page(by_position(kc_ref).astype(BF16), by_position(vc_ref).astype(BF16))

    @pl.when(p == n_pages)
    def _():
        pad = jnp.zeros((PAGE - t_new, N_HEADS * D_VHEAD), F32)
        kb = jnp.concatenate([by_position(kn_ref), pad], axis=0).astype(BF16)
        vb = jnp.concatenate([by_position(vn_ref), pad], axis=0).astype(BF16)
        page(kb, vb)
        lam = _lambda_full(lq1_ref, lk1_ref, lq2_ref, lk2_ref)
        linv = 1.0 / l_scr[...]
        rph = 2 * t_new
        for h in range(N_HEADS):
            a = acc_scr[h] * linv[h * rph:(h + 1) * rph]
            o = a[:t_new] - lam * a[t_new:]
            o_ref[0, :, h * D_VHEAD:(h + 1) * D_VHEAD] = _head_norm(o, gh_ref[...])


def _attn_sample(page_table, q, cache_k, cache_v, k_new, v_new, bias, lams, gh):
    n_seq, t_new, width = q.shape
    n_pages = page_table.shape[1]
    last = n_pages - 1
    seq_blk = pl.BlockSpec((1, t_new, width), lambda s, p, pt: (s, 0, 0))
    new_blk = pl.BlockSpec((t_new, N_HEADS, D_VHEAD), lambda s, p, pt: (s, 0, 0))
    page_blk = pl.BlockSpec((PAGE, N_HEADS, D_VHEAD),
                            lambda s, p, pt: (pt[s, jnp.minimum(p, last)], 0, 0))
    vec = pl.BlockSpec((1, D_HEAD), lambda s, p, pt: (0, 0))

    def bias_idx(s, p, pt):
        return (jnp.maximum(p - (last - 1), 0), 0, 0)

    grid_spec = pltpu.PrefetchScalarGridSpec(
        num_scalar_prefetch=1,
        grid=(n_seq, n_pages + 1),
        in_specs=[
            seq_blk, page_blk, page_blk, new_blk, new_blk,
            pl.BlockSpec((1, ROWS_S, PAGE), bias_idx),
            vec, vec, vec, vec,
            pl.BlockSpec((1, D_VHEAD), lambda s, p, pt: (0, 0)),
        ],
        out_specs=seq_blk,
        scratch_shapes=[pltpu.VMEM((ROWS_S, width), BF16),
                        pltpu.VMEM((ROWS_S, 1), F32),
                        pltpu.VMEM((ROWS_S, 1), F32),
                        pltpu.VMEM((N_HEADS, 2 * t_new, D_VHEAD), F32)],
    )
    return pl.pallas_call(
        functools.partial(_attn_sample_kernel, n_pages=n_pages, t_new=t_new),
        grid_spec=grid_spec,
        out_shape=jax.ShapeDtypeStruct((n_seq, t_new, width), F32),
        compiler_params=pltpu.CompilerParams(
            dimension_semantics=("parallel", "arbitrary"),
            vmem_limit_bytes=VMEM_LIMIT),
        name="attn_sample",
    )(page_table, q, cache_k, cache_v, k_new, v_new, bias, *lams, gh)


def _mix_kernel(sv_ref, u_ref, oa_ref, ga_ref, gm_ref, w_ref, b_ref, o_ref):
    for g in range(GROUPS):
        cs = slice(g * GDIM, (g + 1) * GDIM)
        mixed = jnp.dot(w_ref[0, g], sv_ref[:, cs].astype(BF16),
                        preferred_element_type=F32) + b_ref[0, :, cs]
        merged = ga_ref[:, cs] * oa_ref[:, cs] + gm_ref[:, cs] * (u_ref[:, cs] * mixed)
        o_ref[:, cs] = merged.astype(BF16)


def _mix(sv, u, o_attn, gates, w_mix, b_mix, *, prompt_chunks):
    rows = sv.shape[0]
    blk = pl.BlockSpec((CHUNK, D_MODEL), lambda c: (c, 0))
    kind = lambda c: jnp.where(c >= prompt_chunks, 1, 0)
    return pl.pallas_call(
        _mix_kernel,
        grid=(rows // CHUNK,),
        in_specs=[
            blk, blk, blk,
            pl.BlockSpec((CHUNK, D_MODEL), lambda c: (c, 0)),
            pl.BlockSpec((CHUNK, D_MODEL), lambda c: (c, 1)),
            pl.BlockSpec((1, GROUPS, CHUNK, CHUNK), lambda c: (kind(c), 0, 0, 0)),
            pl.BlockSpec((1, CHUNK, D_MODEL), lambda c: (kind(c), 0, 0)),
        ],
        out_specs=blk,
        out_shape=jax.ShapeDtypeStruct((rows, D_MODEL), BF16),
        compiler_params=pltpu.CompilerParams(
            dimension_semantics=("parallel",),
            vmem_limit_bytes=VMEM_LIMIT),
        name="gmlp_mix_merge",
    )(sv, u, o_attn, gates, gates, w_mix, b_mix)


def _out_proj_kernel(x_ref, w_ref, g_ref, h_ref, o_ref):
    z = jnp.dot(x_ref[...], w_ref[...], preferred_element_type=F32)
    o_ref[...] = h_ref[...] + _rms_rows(z, g_ref[...])


def _out_proj(merged, w_o, g_post, h):
    rows = merged.shape[0]
    blk = pl.BlockSpec((TM_PROJ, D_MODEL), lambda i: (i, 0))
    return pl.pallas_call(
        _out_proj_kernel,
        grid=(rows // TM_PROJ,),
        in_specs=[blk,
                  pl.BlockSpec((D_MODEL, D_MODEL), lambda i: (0, 0)),
                  pl.BlockSpec((1, D_MODEL), lambda i: (0, 0)),
                  blk],
        out_specs=blk,
        out_shape=jax.ShapeDtypeStruct((rows, D_MODEL), F32),
        compiler_params=pltpu.CompilerParams(
            dimension_semantics=("parallel",),
            vmem_limit_bytes=VMEM_LIMIT),
        name="out_proj",
    )(merged, w_o, g_post, h)


def kernel(x_prompt, x_sample, cache_k, cache_v, page_table, rel_bias, ffn1_norm_pre, ffn1_w_gate, ffn1_w_up, ffn1_w_down, ffn1_norm_post, mix_norm_pre, w_in, lambda_q1, lambda_k1, lambda_q2, lambda_k2, attn_head_norm, gmlp_ln_g, gmlp_ln_b, gmlp_w_s, gmlp_b_s, w_o, mix_norm_post, ffn2_norm_pre, ffn2_w_gate, ffn2_w_up, ffn2_w_down, ffn2_norm_post):
    batch, seq, _ = x_prompt.shape
    n_seq, t_new, _ = x_sample.shape
    depth = cache_k.shape[0]
    assert depth == 1
    rows_p = batch * seq
    rows_s = n_seq * t_new
    rows = rows_p + rows_s
    n_pages = page_table.shape[1]
    assert t_new * 2 * N_HEADS == ROWS_S and CHUNK % t_new == 0

    row = lambda a: a.reshape(1, -1).astype(F32)
    bf = lambda a: a[0].astype(BF16)

    x = jnp.concatenate([x_prompt.reshape(rows_p, D_MODEL), x_sample.reshape(rows_s, D_MODEL)], axis=0)

    h1, n1 = _ffn(x, row(ffn1_norm_pre), bf(ffn1_w_gate), bf(ffn1_w_up), bf(ffn1_w_down),
                  row(ffn1_norm_post), row(mix_norm_pre), row0=0, rows=rows, emit_next=True)

    w_in_b = bf(w_in)
    (q_p,) = _proj(n1, w_in_b, sec0=0, nsec=1, row0=0, rows=rows_p, mode="q_bf16")
    (q_s,) = _proj(n1, w_in_b, sec0=0, nsec=1, row0=rows_p, rows=rows_s, mode="q_f32")
    k_p, k_pb = _proj(n1, w_in_b, sec0=1, nsec=1, row0=0, rows=rows_p, mode="kv")
    k_s, _ = _proj(n1, w_in_b, sec0=1, nsec=1, row0=rows_p, rows=rows_s, mode="kv")
    v_p, v_pb = _proj(n1, w_in_b, sec0=2, nsec=1, row0=0, rows=rows_p, mode="kv")
    v_s, _ = _proj(n1, w_in_b, sec0=2, nsec=1, row0=rows_p, rows=rows_s, mode="kv")
    (u,) = _proj(n1, w_in_b, sec0=3, nsec=1, row0=0, rows=rows, mode="gelu")
    (sv,) = _proj(n1, w_in_b, sec0=4, nsec=1, row0=0, rows=rows, mode="gelu_ln",
                  extra=(row(gmlp_ln_g), row(gmlp_ln_b)))
    (gates,) = _proj(n1, w_in_b, sec0=5, nsec=2, row0=0, rows=rows, mode="sigmoid")

    nmax = 2 * T_ATT
    tbl = _bias_by_distance(rel_bias, nmax)
    tbl = (tbl - rel_bias[N_BUCKETS - 1][None, :].astype(F32)).T
    ii = jnp.arange(T_ATT)
    dist = jnp.clip(jnp.arange(2)[:, None, None] * T_ATT + ii[None, :, None] - ii[None, None, :],
                    0, nmax - 1)
    bias_tiles = jnp.take(tbl, dist, axis=1)

    tq = jnp.arange(t_new)
    jj = jnp.arange(PAGE)
    dist_last = jnp.clip(PAGE + tq[:, None] - jj[None, :], 0, nmax - 1)
    b_last = jnp.take(tbl, dist_last, axis=1)
    dist_new = tq[:, None] - jj[None, :]
    b_new = jnp.where((dist_new >= 0) & (jj[None, :] < t_new),
                      jnp.take(tbl, jnp.clip(dist_new, 0, nmax - 1), axis=1), -jnp.inf)
    expand = lambda b: jnp.broadcast_to(b[:, None], (N_HEADS, 2, t_new, PAGE)).reshape(ROWS_S, PAGE)
    bias_s = jnp.stack([jnp.zeros((ROWS_S, PAGE), F32), expand(b_last), expand(b_new)])

    lams = (row(lambda_q1), row(lambda_k1), row(lambda_q2), row(lambda_k2))
    gh = row(attn_head_norm)

    oa_p = _attn_prompt(q_p, k_pb, v_pb, bias_tiles, lams, gh, batch=batch, seq=seq)
    width = N_HEADS * D_VHEAD
    oa_s = _attn_sample(page_table,
                        q_s.reshape(n_seq, t_new, width),
                        cache_k.reshape(-1, N_HEADS, D_VHEAD),
                        cache_v.reshape(-1, N_HEADS, D_VHEAD),
                        k_s, v_s, bias_s, lams, gh)
    o_attn = jnp.concatenate([oa_p, oa_s.reshape(rows_s, width)], axis=0)

    w_s = gmlp_w_s[0].astype(F32)
    w_prompt = jnp.tril(w_s)
    w_small = jnp.tril(w_s[:, :t_new, :t_new])
    eye = jnp.eye(CHUNK // t_new, dtype=F32)
    w_sample = jnp.einsum("ab,gts->gatbs", eye, w_small).reshape(GROUPS, CHUNK, CHUNK)
    w_mix = jnp.stack([w_prompt, w_sample]).astype(BF16)
    b_s = gmlp_b_s[0].astype(F32)
    b_prompt = jnp.repeat(b_s.T, GDIM, axis=1)
    b_sample = jnp.tile(jnp.repeat(b_s[:, :t_new].T, GDIM, axis=1), (CHUNK // t_new, 1))
    b_mix = jnp.stack([b_prompt, b_sample])

    merged = _mix(sv, u, o_attn, gates, w_mix, b_mix, prompt_chunks=rows_p // CHUNK)
    h2 = _out_proj(merged, bf(w_o), row(mix_norm_post), h1)

    f2 = (row(ffn2_norm_pre), bf(ffn2_w_gate), bf(ffn2_w_up), bf(ffn2_w_down),
          row(ffn2_norm_post), row(ffn2_norm_post))
    (y_p,) = _ffn(h2, *f2, row0=0, rows=rows_p, emit_next=False)
    (y_s,) = _ffn(h2, *f2, row0=rows_p, rows=rows_s, emit_next=False)

    return (y_p.reshape(batch, seq, D_MODEL),
            y_s.reshape(n_seq, t_new, D_MODEL),
            k_p.reshape(1, batch, seq, N_HEADS, 2 * D_HEAD),
            v_p.reshape(1, batch, seq, N_HEADS, D_VHEAD),
            k_s.reshape(1, n_seq, t_new, N_HEADS, 2 * D_HEAD),
            v_s.reshape(1, n_seq, t_new, N_HEADS, D_VHEAD),
            sv[rows_p:].reshape(1, n_seq, t_new, D_MODEL))
```

```python
import functools
import math

import jax
import jax.numpy as jnp
from jax import lax
from jax.experimental import pallas as pl
from jax.experimental.pallas import tpu as pltpu

F32 = jnp.float32
BF16 = jnp.bfloat16

D_MODEL = 2048
D_FF = 5632
N_HEADS = 8
D_HEAD = 128
D_VHEAD = 256
SEC_W = 2048
N_BUCKETS = 32
MAX_DISTANCE = 128
PAGE = 128
CHUNK = 128
GROUPS = 16
GDIM = 128
EPS = 1e-6
SCALE = D_HEAD ** -0.5
LOG2E = math.log2(math.e)
QSCALE = SCALE * LOG2E
LAM_INIT = 0.8 - 0.6 * math.exp(-0.3 * 0)

VMEM_LIMIT = 56 * 1024 * 1024
VMEM_LIMIT_FFN = 60 * 1024 * 1024

TM_FFN = 1024
FFN_SPLIT = 2
TF_FFN = 512
TM_PROJ = 1024
PROJ_SPLIT = 4
TM_OUT = 512
T_ATT = 512


def _rms_rows(x, g):
    return x * lax.rsqrt(jnp.mean(x * x, axis=-1, keepdims=True) + EPS) * g


def _ffn_kernel(x_ref, gpre_ref, wg_ref, wu_ref, wd_ref, gpost_ref, gnext_ref,
                h_ref, *rest, emit_next):
    if emit_next:
        n_ref, xn_scr = rest
    else:
        (xn_scr,) = rest
    j = pl.program_id(1)

    @pl.when(j == 0)
    def _():
        xn_scr[...] = _rms_rows(x_ref[...], gpre_ref[...]).astype(BF16)
        h_ref[...] = jnp.zeros(h_ref.shape, F32)

    wg, wu, wd = wg_ref[...], wu_ref[...], wd_ref[...]
    sub = TM_FFN // FFN_SPLIT
    for r in range(FFN_SPLIT):
        rs = slice(r * sub, (r + 1) * sub)
        xn = xn_scr[rs]
        g = jnp.dot(xn, wg, preferred_element_type=F32)
        u = jnp.dot(xn, wu, preferred_element_type=F32)
        a = (g * jax.nn.sigmoid(g) * u).astype(BF16)
        h_ref[rs] += jnp.dot(a, wd, preferred_element_type=F32)

    @pl.when(j == pl.num_programs(1) - 1)
    def _():
        h = x_ref[...] + 0.5 * _rms_rows(h_ref[...], gpost_ref[...])
        h_ref[...] = h
        if emit_next:
            n_ref[...] = _rms_rows(h, gnext_ref[...]).astype(BF16)


def _ffn(x, gpre, wg, wu, wd, gpost, gnext, *, row0, rows, emit_next):
    nb = rows // TM_FFN
    b0 = row0 // TM_FFN
    nf = D_FF // TF_FFN
    vec = pl.BlockSpec((1, D_MODEL), lambda i, j: (0, 0))
    row_blk = pl.BlockSpec((TM_FFN, D_MODEL), lambda i, j: (i, 0))
    out_shape = [jax.ShapeDtypeStruct((rows, D_MODEL), F32)]
    out_specs = [row_blk]
    if emit_next:
        out_shape.append(jax.ShapeDtypeStruct((rows, D_MODEL), BF16))
        out_specs.append(row_blk)
    res = pl.pallas_call(
        functools.partial(_ffn_kernel, emit_next=emit_next),
        grid=(nb, nf),
        in_specs=[
            pl.BlockSpec((TM_FFN, D_MODEL), lambda i, j: (i + b0, 0),
                         pipeline_mode=pl.Buffered(1)),
            vec,
            pl.BlockSpec((D_MODEL, TF_FFN), lambda i, j: (0, j)),
            pl.BlockSpec((D_MODEL, TF_FFN), lambda i, j: (0, j)),
            pl.BlockSpec((TF_FFN, D_MODEL), lambda i, j: (j, 0)),
            vec,
            vec,
        ],
        out_specs=out_specs,
        out_shape=out_shape,
        scratch_shapes=[pltpu.VMEM((TM_FFN, D_MODEL), BF16)],
        compiler_params=pltpu.CompilerParams(
            dimension_semantics=("parallel", "arbitrary"),
            vmem_limit_bytes=VMEM_LIMIT_FFN),
        name="ffn_emit_next" if emit_next else "ffn",
    )(x, gpre, wg, wu, wd, gpost, gnext)
    return res


def _gelu(x):
    return 0.5 * x * (1.0 + lax.erf(x * (1.0 / math.sqrt(2.0))))


def _proj_kernel(n_ref, w_ref, *rest, mode):
    w = w_ref[...]
    sub = TM_PROJ // PROJ_SPLIT
    for r in range(PROJ_SPLIT):
        rs = slice(r * sub, (r + 1) * sub)
        z = jnp.dot(n_ref[rs], w, preferred_element_type=F32)
        if mode == "q_bf16":
            (o_ref,) = rest
            o_ref[rs] = (z * QSCALE).astype(BF16)
        elif mode == "q_f32":
            (o_ref,) = rest
            o_ref[rs] = z * QSCALE
        elif mode == "kv":
            o_ref, ob_ref = rest
            for h in range(N_HEADS):
                o_ref[rs, h, :] = z[:, h * D_VHEAD:(h + 1) * D_VHEAD]
            ob_ref[rs] = z.astype(BF16)
        elif mode == "gelu":
            (o_ref,) = rest
            o_ref[rs] = _gelu(z)
        elif mode == "gelu_ln":
            g_ref, b_ref, o_ref = rest
            a = _gelu(z)
            mu = jnp.mean(a, axis=-1, keepdims=True)
            ac = a - mu
            y = ac * lax.rsqrt(jnp.mean(ac * ac, axis=-1, keepdims=True) + EPS)
            o_ref[rs] = y * g_ref[...] + b_ref[...]
        elif mode == "sigmoid":
            (o_ref,) = rest
            o_ref[rs] = jax.nn.sigmoid(z)
        else:
            raise ValueError(mode)


def _proj(n, w_in, *, sec0, nsec, row0, rows, mode, extra=()):
    nb = rows // TM_PROJ
    b0 = row0 // TM_PROJ
    blk = pl.BlockSpec((TM_PROJ, SEC_W), lambda s, i: (i, s))
    if mode == "q_bf16":
        out_dtypes = [BF16]
    elif mode == "kv":
        out_dtypes = [F32, BF16]
    else:
        out_dtypes = [F32]
    out_shape = [jax.ShapeDtypeStruct((rows, nsec * SEC_W), dt) for dt in out_dtypes]
    out_specs = [blk] * len(out_dtypes)
    if mode == "kv":
        assert nsec == 1
        out_shape[0] = jax.ShapeDtypeStruct((rows, N_HEADS, D_VHEAD), F32)
        out_specs[0] = pl.BlockSpec((TM_PROJ, N_HEADS, D_VHEAD), lambda s, i: (i, 0, 0))
    vec = pl.BlockSpec((1, SEC_W), lambda s, i: (0, 0))
    res = pl.pallas_call(
        functools.partial(_proj_kernel, mode=mode),
        grid=(nsec, nb),
        in_specs=[
            pl.BlockSpec((TM_PROJ, D_MODEL), lambda s, i: (i + b0, 0)),
            pl.BlockSpec((D_MODEL, SEC_W), lambda s, i: (0, s + sec0)),
        ] + [vec] * len(extra),
        out_specs=out_specs,
        out_shape=out_shape,
        compiler_params=pltpu.CompilerParams(
            dimension_semantics=("parallel", "parallel"),
            vmem_limit_bytes=VMEM_LIMIT),
        name="proj_" + mode,
    )(n, w_in, *extra)
    return res


def _lambda_full(lq1_ref, lk1_ref, lq2_ref, lk2_ref):
    s1 = jnp.sum(lq1_ref[...] * lk1_ref[...], axis=-1, keepdims=True)
    s2 = jnp.sum(lq2_ref[...] * lk2_ref[...], axis=-1, keepdims=True)
    return jnp.exp(s1) - jnp.exp(s2) + LAM_INIT


def _head_norm(o, gh):
    return _rms_rows(o, gh) * (1.0 - LAM_INIT)


def _bias_by_distance(rel_bias, nmax):
    n = jnp.arange(nmax)
    max_exact = N_BUCKETS // 2
    nf = jnp.maximum(n, 1).astype(F32)
    large = max_exact + (jnp.log(nf / max_exact) / math.log(MAX_DISTANCE / max_exact)
                         * (N_BUCKETS - max_exact)).astype(jnp.int32)
    large = jnp.minimum(large, N_BUCKETS - 1)
    bucket = jnp.where(n < max_exact, n, large)
    return rel_bias[bucket].astype(F32)


SUB = 128


def _attn_prompt_kernel(qi_tab, ki_tab, q_ref, k_ref, v_ref, dt_ref,
                        lq1_ref, lk1_ref, lq2_ref, lk2_ref, gh_ref,
                        o_ref, m_scr, l_scr, acc_scr, bias_scr):
    t = pl.program_id(2)
    qi = qi_tab[t]
    ki = ki_tab[t]
    d = qi - ki

    @pl.when(t == 0)
    def _():
        d0 = dt_ref[0, 0]
        d1 = dt_ref[0, 1]
        neg = jnp.full((SUB, SUB), -jnp.inf, F32)
        zero = jnp.zeros((SUB, SUB), F32)
        nsub = T_ATT // SUB
        for a in range(nsub):
            for b in range(nsub):
                rs, cs = slice(a * SUB, (a + 1) * SUB), slice(b * SUB, (b + 1) * SUB)
                bias_scr[0, rs, cs] = d0 if a == b else d1 if a == b + 1 else neg if a < b else zero
                bias_scr[1, rs, cs] = d1 if (a == 0 and b == nsub - 1) else zero

    @pl.when(ki == 0)
    def _():
        m_scr[...] = jnp.full(m_scr.shape, -jnp.inf, F32)
        l_scr[...] = jnp.zeros(l_scr.shape, F32)
        acc_scr[...] = jnp.zeros(acc_scr.shape, F32)

    q = q_ref[...]
    k = k_ref[...]
    v = v_ref[...]
    nt = (((1,), (1,)), ((), ()))

    def scores(mi):
        sl = slice(mi * D_HEAD, (mi + 1) * D_HEAD)
        return lax.dot_general(q[:, sl], k[:, sl], nt, preferred_element_type=F32)

    def update(mi, s):
        m_old = m_scr[mi]
        m_new = jnp.maximum(m_old, jnp.max(s, axis=-1, keepdims=True))
        alpha = jnp.exp2(m_old - m_new)
        p = jnp.exp2(s - m_new)
        l_scr[mi] = alpha * l_scr[mi] + jnp.sum(p, axis=-1, keepdims=True)
        acc_scr[mi] = alpha * acc_scr[mi] + jnp.dot(p.astype(BF16), v,
                                                    preferred_element_type=F32)
        m_scr[mi] = m_new

    @pl.when(d >= 2)
    def _():
        for mi in range(2):
            update(mi, scores(mi))

    @pl.when(d < 2)
    def _():
        for mi in range(2):
            update(mi, scores(mi) + bias_scr[d])

    @pl.when(d == 0)
    def _():
        lam = _lambda_full(lq1_ref, lk1_ref, lq2_ref, lk2_ref)
        o = acc_scr[0] / l_scr[0] - lam * (acc_scr[1] / l_scr[1])
        o_ref[...] = _head_norm(o, gh_ref[...])


def _attn_prompt(q, k, v, dtiles, lams, gh, *, batch, seq):
    nq = seq // T_ATT
    tri = [(qi, ki) for qi in range(nq) for ki in range(qi + 1)]
    qi_tab = jnp.asarray([a for a, _ in tri], jnp.int32)
    ki_tab = jnp.asarray([b for _, b in tri], jnp.int32)
    vec = pl.BlockSpec((1, D_HEAD), lambda b, h, t, qt, kt: (0, 0))
    grid_spec = pltpu.PrefetchScalarGridSpec(
        num_scalar_prefetch=2,
        grid=(batch, N_HEADS, len(tri)),
        in_specs=[
            pl.BlockSpec((T_ATT, 2 * D_HEAD), lambda b, h, t, qt, kt: (b * nq + qt[t], h)),
            pl.BlockSpec((T_ATT, 2 * D_HEAD), lambda b, h, t, qt, kt: (b * nq + kt[t], h)),
            pl.BlockSpec((T_ATT, D_VHEAD), lambda b, h, t, qt, kt: (b * nq + kt[t], h)),
            pl.BlockSpec((1, 2, SUB, SUB), lambda b, h, t, qt, kt: (h, 0, 0, 0)),
            vec, vec, vec, vec,
            pl.BlockSpec((1, D_VHEAD), lambda b, h, t, qt, kt: (0, 0)),
        ],
        out_specs=pl.BlockSpec((T_ATT, D_VHEAD), lambda b, h, t, qt, kt: (b * nq + qt[t], h)),
        scratch_shapes=[pltpu.VMEM((2, T_ATT, 1), F32),
                        pltpu.VMEM((2, T_ATT, 1), F32),
                        pltpu.VMEM((2, T_ATT, D_VHEAD), F32),
                        pltpu.VMEM((2, T_ATT, T_ATT), F32)],
    )
    return pl.pallas_call(
        _attn_prompt_kernel,
        grid_spec=grid_spec,
        out_shape=jax.ShapeDtypeStruct((batch * seq, N_HEADS * D_VHEAD), F32),
        compiler_params=pltpu.CompilerParams(
            dimension_semantics=("parallel", "parallel", "arbitrary"),
            vmem_limit_bytes=VMEM_LIMIT),
        name="attn_prompt",
    )(qi_tab, ki_tab, q, k, v, dtiles, *lams, gh)


ROWS_S = 128
PAGES_PER_STEP = 4


def _attn_sample_kernel(pt_ref, q_ref, *refs, n_steps, t_new):
    pps = PAGES_PER_STEP
    page_refs = refs[:4 * pps]
    (knl_ref, knh_ref, vnl_ref, vnh_ref, bias_ref, bnew_ref,
     lq1_ref, lk1_ref, lq2_ref, lk2_ref, gh_ref,
     o_ref, qh_scr, m_scr, l_scr, acc_scr) = refs[4 * pps:]
    p = pl.program_id(1)
    rph = 2 * t_new
    nt = (((1,), (1,)), ((), ()))

    @pl.when(p == 0)
    def _():
        q = q_ref[0]
        zero = jnp.zeros((t_new, D_HEAD), F32)
        for h in range(N_HEADS):
            q1 = q[:, h * 2 * D_HEAD:h * 2 * D_HEAD + D_HEAD]
            q2 = q[:, h * 2 * D_HEAD + D_HEAD:(h + 1) * 2 * D_HEAD]
            qh_scr[h] = jnp.concatenate(
                [jnp.concatenate([q1, zero], axis=1), jnp.concatenate([zero, q2], axis=1)],
                axis=0).astype(BF16)
        m_scr[...] = jnp.full(m_scr.shape, -jnp.inf, F32)
        l_scr[...] = jnp.zeros(l_scr.shape, F32)
        acc_scr[...] = jnp.zeros(acc_scr.shape, F32)

    def head_rows(lo_ref, hi_ref, h, n):
        rows = pl.ds(h, n, stride=N_HEADS)
        return jnp.concatenate([lo_ref[rows, :], hi_ref[rows, :]], axis=1)

    def attend(kh, vh, bias):
        s = jnp.concatenate(
            [lax.dot_general(qh_scr[h], kh[h], nt, preferred_element_type=F32)
             for h in range(N_HEADS)], axis=0) + bias
        m_old = m_scr[...]
        m_new = jnp.maximum(m_old, jnp.max(s, axis=-1, keepdims=True))
        alpha = jnp.exp2(m_old - m_new)
        pr = jnp.exp2(s - m_new)
        l_scr[...] = alpha * l_scr[...] + jnp.sum(pr, axis=-1, keepdims=True)
        m_scr[...] = m_new
        prb = pr.astype(BF16)
        for h in range(N_HEADS):
            rs = slice(h * rph, (h + 1) * rph)
            pv = jnp.dot(prb[rs], vh[h], preferred_element_type=F32)
            acc_scr[h] = alpha[rs] * acc_scr[h] + pv

    kh, vh = [], []
    for h in range(N_HEADS):
        kh.append(jnp.concatenate(
            [head_rows(page_refs[4 * r], page_refs[4 * r + 1], h, PAGE) for r in range(pps)],
            axis=0).astype(BF16))
        vh.append(jnp.concatenate(
            [head_rows(page_refs[4 * r + 2], page_refs[4 * r + 3], h, PAGE) for r in range(pps)],
            axis=0).astype(BF16))
    attend(kh, vh, bias_ref[0])

    @pl.when(p == n_steps - 1)
    def _():
        pad = jnp.zeros((PAGE - t_new, D_VHEAD), F32)
        kn = [jnp.concatenate([head_rows(knl_ref, knh_ref, h, t_new), pad], axis=0).astype(BF16)
              for h in range(N_HEADS)]
        vn = [jnp.concatenate([head_rows(vnl_ref, vnh_ref, h, t_new), pad], axis=0).astype(BF16)
              for h in range(N_HEADS)]
        attend(kn, vn, bnew_ref[...])
        lam = _lambda_full(lq1_ref, lk1_ref, lq2_ref, lk2_ref)
        linv = 1.0 / l_scr[...]
        for h in range(N_HEADS):
            a = acc_scr[h] * linv[h * rph:(h + 1) * rph]
            o = a[:t_new] - lam * a[t_new:]
            o_ref[0, :, h * D_VHEAD:(h + 1) * D_VHEAD] = _head_norm(o, gh_ref[...])


def _attn_sample(page_table, q, cache_k, cache_v, k_new, v_new, bias, bias_new, lams, gh):
    n_seq, t_new, width = q.shape
    pps = PAGES_PER_STEP
    n_steps = page_table.shape[1] // pps
    assert n_steps * pps == page_table.shape[1]
    seq_blk = pl.BlockSpec((1, t_new, width), lambda s, p, pt: (s, 0, 0))
    vec = pl.BlockSpec((1, D_HEAD), lambda s, p, pt: (0, 0))

    def half_blk(rows, index, half):
        return pl.BlockSpec((rows * N_HEADS, D_HEAD), lambda s, p, pt: (index(s, p, pt), half))

    page_specs, page_args = [], []
    for r in range(pps):
        idx = lambda s, p, pt, r=r: pt[s, p * pps + r]
        for arr in (cache_k, cache_v):
            for half in range(2):
                page_specs.append(half_blk(PAGE, idx, half))
                page_args.append(arr)
    new_idx = lambda s, p, pt: s
    new_specs = [half_blk(t_new, new_idx, half) for _ in range(2) for half in range(2)]

    grid_spec = pltpu.PrefetchScalarGridSpec(
        num_scalar_prefetch=1,
        grid=(n_seq, n_steps),
        in_specs=[seq_blk] + page_specs + new_specs + [
            pl.BlockSpec((1, ROWS_S, pps * PAGE),
                         lambda s, p, pt: (jnp.where(p == n_steps - 1, 1, 0), 0, 0)),
            pl.BlockSpec((ROWS_S, PAGE), lambda s, p, pt: (0, 0)),
            vec, vec, vec, vec,
            pl.BlockSpec((1, D_VHEAD), lambda s, p, pt: (0, 0)),
        ],
        out_specs=seq_blk,
        scratch_shapes=[pltpu.VMEM((N_HEADS, 2 * t_new, 2 * D_HEAD), BF16),
                        pltpu.VMEM((ROWS_S, 1), F32),
                        pltpu.VMEM((ROWS_S, 1), F32),
                        pltpu.VMEM((N_HEADS, 2 * t_new, D_VHEAD), F32)],
    )
    return pl.pallas_call(
        functools.partial(_attn_sample_kernel, n_steps=n_steps, t_new=t_new),
        grid_spec=grid_spec,
        out_shape=jax.ShapeDtypeStruct((n_seq, t_new, width), F32),
        compiler_params=pltpu.CompilerParams(
            dimension_semantics=("parallel", "arbitrary"),
            vmem_limit_bytes=VMEM_LIMIT),
        name="attn_sample",
    )(page_table, q, *page_args, k_new, k_new, v_new, v_new, bias, bias_new, *lams, gh)


def _mix_kernel(sv_ref, u_ref, oa_ref, ga_ref, gm_ref, w_ref, b_ref, o_ref):
    for g in range(GROUPS):
        cs = slice(g * GDIM, (g + 1) * GDIM)
        mixed = jnp.dot(w_ref[0, g], sv_ref[:, cs].astype(BF16),
                        preferred_element_type=F32) + b_ref[0, :, cs]
        merged = ga_ref[:, cs] * oa_ref[:, cs] + gm_ref[:, cs] * (u_ref[:, cs] * mixed)
        o_ref[:, cs] = merged.astype(BF16)


def _mix(sv, u, o_attn, gates, w_mix, b_mix, *, prompt_chunks):
    rows = sv.shape[0]
    blk = pl.BlockSpec((CHUNK, D_MODEL), lambda c: (c, 0))
    kind = lambda c: jnp.where(c >= prompt_chunks, 1, 0)
    return pl.pallas_call(
        _mix_kernel,
        grid=(rows // CHUNK,),
        in_specs=[
            blk, blk, blk,
            pl.BlockSpec((CHUNK, D_MODEL), lambda c: (c, 0)),
            pl.BlockSpec((CHUNK, D_MODEL), lambda c: (c, 1)),
            pl.BlockSpec((1, GROUPS, CHUNK, CHUNK), lambda c: (kind(c), 0, 0, 0)),
            pl.BlockSpec((1, CHUNK, D_MODEL), lambda c: (kind(c), 0, 0)),
        ],
        out_specs=blk,
        out_shape=jax.ShapeDtypeStruct((rows, D_MODEL), BF16),
        compiler_params=pltpu.CompilerParams(
            dimension_semantics=("parallel",),
            vmem_limit_bytes=VMEM_LIMIT),
        name="gmlp_mix_merge",
    )(sv, u, o_attn, gates, gates, w_mix, b_mix)


def _out_proj_kernel(x_ref, w_ref, g_ref, h_ref, o_ref):
    w = w_ref[...]
    sub = TM_OUT // 2
    for r in range(2):
        rs = slice(r * sub, (r + 1) * sub)
        z = jnp.dot(x_ref[rs], w, preferred_element_type=F32)
        o_ref[rs] = h_ref[rs] + _rms_rows(z, g_ref[...])


def _out_proj(merged, w_o, g_post, h):
    rows = merged.shape[0]
    blk = pl.BlockSpec((TM_OUT, D_MODEL), lambda i: (i, 0))
    return pl.pallas_call(
        _out_proj_kernel,
        grid=(rows // TM_OUT,),
        in_specs=[blk,
                  pl.BlockSpec((D_MODEL, D_MODEL), lambda i: (0, 0)),
                  pl.BlockSpec((1, D_MODEL), lambda i: (0, 0)),
                  blk],
        out_specs=blk,
        out_shape=jax.ShapeDtypeStruct((rows, D_MODEL), F32),
        compiler_params=pltpu.CompilerParams(
            dimension_semantics=("parallel",),
            vmem_limit_bytes=VMEM_LIMIT),
        name="out_proj",
    )(merged, w_o, g_post, h)


def kernel(x_prompt, x_sample, cache_k, cache_v, page_table, rel_bias, ffn1_norm_pre, ffn1_w_gate, ffn1_w_up, ffn1_w_down, ffn1_norm_post, mix_norm_pre, w_in, lambda_q1, lambda_k1, lambda_q2, lambda_k2, attn_head_norm, gmlp_ln_g, gmlp_ln_b, gmlp_w_s, gmlp_b_s, w_o, mix_norm_post, ffn2_norm_pre, ffn2_w_gate, ffn2_w_up, ffn2_w_down, ffn2_norm_post):
    batch, seq, _ = x_prompt.shape
    n_seq, t_new, _ = x_sample.shape
    depth = cache_k.shape[0]
    assert depth == 1
    rows_p = batch * seq
    rows_s = n_seq * t_new
    rows = rows_p + rows_s
    assert t_new * 2 * N_HEADS == ROWS_S and CHUNK % t_new == 0

    row = lambda a: a.reshape(1, -1).astype(F32)
    bf = lambda a: a[0].astype(BF16)

    x = jnp.concatenate([x_prompt.reshape(rows_p, D_MODEL), x_sample.reshape(rows_s, D_MODEL)], axis=0)

    h1, n1 = _ffn(x, row(ffn1_norm_pre), bf(ffn1_w_gate), bf(ffn1_w_up), bf(ffn1_w_down),
                  row(ffn1_norm_post), row(mix_norm_pre), row0=0, rows=rows, emit_next=True)

    w_in_b = bf(w_in)
    (q_p,) = _proj(n1, w_in_b, sec0=0, nsec=1, row0=0, rows=rows_p, mode="q_bf16")
    (q_s,) = _proj(n1, w_in_b, sec0=0, nsec=1, row0=rows_p, rows=rows_s, mode="q_f32")
    k_p, k_pb = _proj(n1, w_in_b, sec0=1, nsec=1, row0=0, rows=rows_p, mode="kv")
    k_s, _ = _proj(n1, w_in_b, sec0=1, nsec=1, row0=rows_p, rows=rows_s, mode="kv")
    v_p, v_pb = _proj(n1, w_in_b, sec0=2, nsec=1, row0=0, rows=rows_p, mode="kv")
    v_s, _ = _proj(n1, w_in_b, sec0=2, nsec=1, row0=rows_p, rows=rows_s, mode="kv")
    (u,) = _proj(n1, w_in_b, sec0=3, nsec=1, row0=0, rows=rows, mode="gelu")
    (sv,) = _proj(n1, w_in_b, sec0=4, nsec=1, row0=0, rows=rows, mode="gelu_ln",
                  extra=(row(gmlp_ln_g), row(gmlp_ln_b)))
    (gates,) = _proj(n1, w_in_b, sec0=5, nsec=2, row0=0, rows=rows, mode="sigmoid")

    tbl = _bias_by_distance(rel_bias, SUB + 1)
    g = ((tbl - rel_bias[N_BUCKETS - 1][None, :].astype(F32)) * LOG2E).T
    vec = jnp.concatenate([g[:, ::-1], jnp.zeros((N_HEADS, SUB), F32)], axis=1)
    a = jnp.tile(vec, (1, SUB))[:, :SUB * 2 * SUB].reshape(N_HEADS, SUB, 2 * SUB)
    ii = jnp.arange(SUB)
    d1 = a[:, :, :SUB]
    d0 = jnp.where(ii[:, None] >= ii[None, :], a[:, :, SUB:], -jnp.inf)
    dtiles = jnp.stack([d0, d1], axis=1)

    expand = lambda b: jnp.broadcast_to(b[:, None], (N_HEADS, 2) + b.shape[1:]).reshape(ROWS_S, -1)
    b_last = expand(d1[:, :t_new, :])
    far = jnp.zeros((ROWS_S, (PAGES_PER_STEP - 1) * PAGE), F32)
    bias_s = jnp.stack([jnp.zeros((ROWS_S, PAGES_PER_STEP * PAGE), F32),
                        jnp.concatenate([far, b_last], axis=1)])
    b_new = jnp.concatenate([d0[:, :t_new, :t_new],
                             jnp.full((N_HEADS, t_new, PAGE - t_new), -jnp.inf, F32)], axis=2)
    bias_new = expand(b_new)

    lams = (row(lambda_q1), row(lambda_k1), row(lambda_q2), row(lambda_k2))
    gh = row(attn_head_norm)

    oa_p = _attn_prompt(q_p, k_pb, v_pb, dtiles, lams, gh, batch=batch, seq=seq)
    width = N_HEADS * D_VHEAD
    oa_s = _attn_sample(page_table,
                        q_s.reshape(n_seq, t_new, width),
                        cache_k.reshape(-1, D_VHEAD),
                        cache_v.reshape(-1, D_VHEAD),
                        k_s.reshape(-1, D_VHEAD), v_s.reshape(-1, D_VHEAD),
                        bias_s, bias_new, lams, gh)
    o_attn = jnp.concatenate([oa_p, oa_s.reshape(rows_s, width)], axis=0)

    w_s = gmlp_w_s[0].astype(F32)
    w_prompt = jnp.tril(w_s)
    w_small = jnp.tril(w_s[:, :t_new, :t_new])
    eye = jnp.eye(CHUNK // t_new, dtype=F32)
    w_sample = jnp.einsum("ab,gts->gatbs", eye, w_small).reshape(GROUPS, CHUNK, CHUNK)
    w_mix = jnp.stack([w_prompt, w_sample]).astype(BF16)
    b_s = gmlp_b_s[0].astype(F32)
    b_prompt = jnp.repeat(b_s.T, GDIM, axis=1)
    b_sample = jnp.tile(jnp.repeat(b_s[:, :t_new].T, GDIM, axis=1), (CHUNK // t_new, 1))
    b_mix = jnp.stack([b_prompt, b_sample])

    merged = _mix(sv, u, o_attn, gates, w_mix, b_mix, prompt_chunks=rows_p // CHUNK)
    h2 = _out_proj(merged, bf(w_o), row(mix_norm_post), h1)

    f2 = (row(ffn2_norm_pre), bf(ffn2_w_gate), bf(ffn2_w_up), bf(ffn2_w_down),
          row(ffn2_norm_post), row(ffn2_norm_post))
    (y_p,) = _ffn(h2, *f2, row0=0, rows=rows_p, emit_next=False)
    (y_s,) = _ffn(h2, *f2, row0=rows_p, rows=rows_s, emit_next=False)

    return (y_p.reshape(batch, seq, D_MODEL),
            y_s.reshape(n_seq, t_new, D_MODEL),
            k_p.reshape(1, batch, seq, N_HEADS, 2 * D_HEAD),
            v_p.reshape(1, batch, seq, N_HEADS, D_VHEAD),
            k_s.reshape(1, n_seq, t_new, N_HEADS, 2 * D_HEAD),
            v_s.reshape(1, n_seq, t_new, N_HEADS, D_VHEAD),
            sv[rows_p:].reshape(1, n_seq, t_new, D_MODEL))
```

```python
import functools
import math

import jax
import jax.numpy as jnp
from jax import lax
from jax.experimental import pallas as pl
from jax.experimental.pallas import tpu as pltpu

F32 = jnp.float32
BF16 = jnp.bfloat16

D_MODEL = 2048
D_FF = 5632
N_HEADS = 8
D_HEAD = 128
D_VHEAD = 256
SEC_W = 2048
N_BUCKETS = 32
MAX_DISTANCE = 128
PAGE = 128
CHUNK = 128
GROUPS = 16
GDIM = 128
EPS = 1e-6
SCALE = D_HEAD ** -0.5
LOG2E = math.log2(math.e)
QSCALE = SCALE * LOG2E
LAM_INIT = 0.8 - 0.6 * math.exp(-0.3 * 0)

VMEM_LIMIT = 56 * 1024 * 1024
VMEM_LIMIT_FFN = 60 * 1024 * 1024

TM_FFN = 1024
FFN_SPLIT = 2
TF_FFN = 512
TM_PROJ = 1024
PROJ_SPLIT = 4
TM_OUT = 512
T_ATT = 512


def _rms_rows(x, g):
    return x * lax.rsqrt(jnp.mean(x * x, axis=-1, keepdims=True) + EPS) * g


def _ffn_kernel(x_ref, gpre_ref, wg_ref, wu_ref, wd_ref, gpost_ref, gnext_ref,
                h_ref, *rest, emit_next):
    if emit_next:
        n_ref, xn_scr = rest
    else:
        (xn_scr,) = rest
    j = pl.program_id(1)

    @pl.when(j == 0)
    def _():
        xn_scr[...] = _rms_rows(x_ref[...], gpre_ref[...]).astype(BF16)
        h_ref[...] = jnp.zeros(h_ref.shape, F32)

    wg, wu, wd = wg_ref[...], wu_ref[...], wd_ref[...]
    sub = TM_FFN // FFN_SPLIT
    for r in range(FFN_SPLIT):
        rs = slice(r * sub, (r + 1) * sub)
        xn = xn_scr[rs]
        g = jnp.dot(xn, wg, preferred_element_type=F32)
        u = jnp.dot(xn, wu, preferred_element_type=F32)
        a = (g * jax.nn.sigmoid(g) * u).astype(BF16)
        h_ref[rs] += jnp.dot(a, wd, preferred_element_type=F32)

    @pl.when(j == pl.num_programs(1) - 1)
    def _():
        h = x_ref[...] + 0.5 * _rms_rows(h_ref[...], gpost_ref[...])
        h_ref[...] = h
        if emit_next:
            n_ref[...] = _rms_rows(h, gnext_ref[...]).astype(BF16)


def _ffn(x, gpre, wg, wu, wd, gpost, gnext, *, row0, rows, emit_next):
    nb = rows // TM_FFN
    b0 = row0 // TM_FFN
    nf = D_FF // TF_FFN
    vec = pl.BlockSpec((1, D_MODEL), lambda i, j: (0, 0))
    row_blk = pl.BlockSpec((TM_FFN, D_MODEL), lambda i, j: (i, 0))
    out_shape = [jax.ShapeDtypeStruct((rows, D_MODEL), F32)]
    out_specs = [row_blk]
    if emit_next:
        out_shape.append(jax.ShapeDtypeStruct((rows, D_MODEL), BF16))
        out_specs.append(row_blk)
    res = pl.pallas_call(
        functools.partial(_ffn_kernel, emit_next=emit_next),
        grid=(nb, nf),
        in_specs=[
            pl.BlockSpec((TM_FFN, D_MODEL), lambda i, j: (i + b0, 0),
                         pipeline_mode=pl.Buffered(1)),
            vec,
            pl.BlockSpec((D_MODEL, TF_FFN), lambda i, j: (0, j)),
            pl.BlockSpec((D_MODEL, TF_FFN), lambda i, j: (0, j)),
            pl.BlockSpec((TF_FFN, D_MODEL), lambda i, j: (j, 0)),
            vec,
            vec,
        ],
        out_specs=out_specs,
        out_shape=out_shape,
        scratch_shapes=[pltpu.VMEM((TM_FFN, D_MODEL), BF16)],
        compiler_params=pltpu.CompilerParams(
            dimension_semantics=("parallel", "arbitrary"),
            vmem_limit_bytes=VMEM_LIMIT_FFN),
        name="ffn_emit_next" if emit_next else "ffn",
    )(x, gpre, wg, wu, wd, gpost, gnext)
    return res


def _gelu(x):
    return 0.5 * x * (1.0 + lax.erf(x * (1.0 / math.sqrt(2.0))))


def _proj_kernel(n_ref, w_ref, *rest, mode):
    w = w_ref[...]
    sub = TM_PROJ // PROJ_SPLIT
    for r in range(PROJ_SPLIT):
        rs = slice(r * sub, (r + 1) * sub)
        z = jnp.dot(n_ref[rs], w, preferred_element_type=F32)
        if mode == "q_bf16":
            (o_ref,) = rest
            o_ref[rs] = (z * QSCALE).astype(BF16)
        elif mode == "q_f32":
            (o_ref,) = rest
            o_ref[rs] = z * QSCALE
        elif mode == "kv":
            o_ref, ob_ref = rest
            for h in range(N_HEADS):
                o_ref[rs, h, :] = z[:, h * D_VHEAD:(h + 1) * D_VHEAD]
            ob_ref[rs] = z.astype(BF16)
        elif mode == "gelu":
            (o_ref,) = rest
            o_ref[rs] = _gelu(z).astype(BF16)
        elif mode == "gelu_ln":
            g_ref, b_ref, o_ref = rest
            a = _gelu(z)
            mu = jnp.mean(a, axis=-1, keepdims=True)
            ac = a - mu
            y = ac * lax.rsqrt(jnp.mean(ac * ac, axis=-1, keepdims=True) + EPS)
            o_ref[rs] = y * g_ref[...] + b_ref[...]
        elif mode == "sigmoid":
            (o_ref,) = rest
            o_ref[rs] = jax.nn.sigmoid(z).astype(BF16)
        else:
            raise ValueError(mode)


def _proj(n, w_in, *, sec0, nsec, row0, rows, mode, extra=()):
    nb = rows // TM_PROJ
    b0 = row0 // TM_PROJ
    blk = pl.BlockSpec((TM_PROJ, SEC_W), lambda s, i: (i, s))
    if mode in ("q_bf16", "gelu", "sigmoid"):
        out_dtypes = [BF16]
    elif mode == "kv":
        out_dtypes = [F32, BF16]
    else:
        out_dtypes = [F32]
    out_shape = [jax.ShapeDtypeStruct((rows, nsec * SEC_W), dt) for dt in out_dtypes]
    out_specs = [blk] * len(out_dtypes)
    if mode == "kv":
        assert nsec == 1
        out_shape[0] = jax.ShapeDtypeStruct((rows, N_HEADS, D_VHEAD), F32)
        out_specs[0] = pl.BlockSpec((TM_PROJ, N_HEADS, D_VHEAD), lambda s, i: (i, 0, 0))
    vec = pl.BlockSpec((1, SEC_W), lambda s, i: (0, 0))
    res = pl.pallas_call(
        functools.partial(_proj_kernel, mode=mode),
        grid=(nsec, nb),
        in_specs=[
            pl.BlockSpec((TM_PROJ, D_MODEL), lambda s, i: (i + b0, 0)),
            pl.BlockSpec((D_MODEL, SEC_W), lambda s, i: (0, s + sec0)),
        ] + [vec] * len(extra),
        out_specs=out_specs,
        out_shape=out_shape,
        compiler_params=pltpu.CompilerParams(
            dimension_semantics=("parallel", "parallel"),
            vmem_limit_bytes=VMEM_LIMIT),
        name="proj_" + mode,
    )(n, w_in, *extra)
    return res


def _lambda_full(lq1_ref, lk1_ref, lq2_ref, lk2_ref):
    s1 = jnp.sum(lq1_ref[...] * lk1_ref[...], axis=-1, keepdims=True)
    s2 = jnp.sum(lq2_ref[...] * lk2_ref[...], axis=-1, keepdims=True)
    return jnp.exp(s1) - jnp.exp(s2) + LAM_INIT


def _head_norm(o, gh):
    return _rms_rows(o, gh) * (1.0 - LAM_INIT)


def _bias_by_distance(rel_bias, nmax):
    n = jnp.arange(nmax)
    max_exact = N_BUCKETS // 2
    nf = jnp.maximum(n, 1).astype(F32)
    large = max_exact + (jnp.log(nf / max_exact) / math.log(MAX_DISTANCE / max_exact)
                         * (N_BUCKETS - max_exact)).astype(jnp.int32)
    large = jnp.minimum(large, N_BUCKETS - 1)
    bucket = jnp.where(n < max_exact, n, large)
    return rel_bias[bucket].astype(F32)


SUB = 128


def _attn_prompt_kernel(qi_tab, ki_tab, q_ref, k_ref, v_ref, dt_ref,
                        lq1_ref, lk1_ref, lq2_ref, lk2_ref, gh_ref,
                        o_ref, m_scr, l_scr, acc_scr, bias_scr):
    t = pl.program_id(2)
    qi = qi_tab[t]
    ki = ki_tab[t]
    d = qi - ki

    @pl.when(t == 0)
    def _():
        d0 = dt_ref[0, 0]
        d1 = dt_ref[0, 1]
        neg = jnp.full((SUB, SUB), -jnp.inf, F32)
        zero = jnp.zeros((SUB, SUB), F32)
        nsub = T_ATT // SUB
        for a in range(nsub):
            for b in range(nsub):
                ks, qs = slice(b * SUB, (b + 1) * SUB), slice(a * SUB, (a + 1) * SUB)
                bias_scr[0, ks, qs] = d0 if a == b else d1 if a == b + 1 else neg if a < b else zero
                bias_scr[1, ks, qs] = d1 if (a == 0 and b == nsub - 1) else zero

    @pl.when(ki == 0)
    def _():
        m_scr[...] = jnp.full(m_scr.shape, -jnp.inf, F32)
        l_scr[...] = jnp.zeros(l_scr.shape, F32)
        acc_scr[...] = jnp.zeros(acc_scr.shape, F32)

    q = q_ref[...]
    k = k_ref[...]
    v = v_ref[...]
    nt = (((1,), (1,)), ((), ()))
    tn = (((0,), (0,)), ((), ()))

    def scores(mi):
        sl = slice(mi * D_HEAD, (mi + 1) * D_HEAD)
        return lax.dot_general(k[:, sl], q[:, sl], nt, preferred_element_type=F32)

    def update(mi, s):
        m_old = m_scr[mi]
        m_new = jnp.maximum(m_old, jnp.max(s, axis=0, keepdims=True))
        alpha = jnp.exp2(m_old - m_new)
        p = jnp.exp2(s - m_new)
        l_scr[mi] = alpha * l_scr[mi] + jnp.sum(p, axis=0, keepdims=True)
        acc_scr[mi] = alpha * acc_scr[mi] + lax.dot_general(
            v, p.astype(BF16), tn, preferred_element_type=F32)
        m_scr[mi] = m_new

    @pl.when(d >= 2)
    def _():
        for mi in range(2):
            update(mi, scores(mi))

    @pl.when(d < 2)
    def _():
        for mi in range(2):
            update(mi, scores(mi) + bias_scr[d])

    @pl.when(d == 0)
    def _():
        lam = _lambda_full(lq1_ref, lk1_ref, lq2_ref, lk2_ref)
        ot = acc_scr[0] / l_scr[0] - lam * (acc_scr[1] / l_scr[1])
        ot = ot * lax.rsqrt(jnp.mean(ot * ot, axis=0, keepdims=True) + EPS)
        o_ref[...] = (jnp.transpose(ot) * (gh_ref[...] * (1.0 - LAM_INIT))).astype(BF16)


def _attn_prompt(q, k, v, dtiles, lams, gh, *, batch, seq):
    nq = seq // T_ATT
    tri = [(qi, ki) for qi in range(nq) for ki in range(qi + 1)]
    qi_tab = jnp.asarray([a for a, _ in tri], jnp.int32)
    ki_tab = jnp.asarray([b for _, b in tri], jnp.int32)
    vec = pl.BlockSpec((1, D_HEAD), lambda b, h, t, qt, kt: (0, 0))
    grid_spec = pltpu.PrefetchScalarGridSpec(
        num_scalar_prefetch=2,
        grid=(batch, N_HEADS, len(tri)),
        in_specs=[
            pl.BlockSpec((T_ATT, 2 * D_HEAD), lambda b, h, t, qt, kt: (b * nq + qt[t], h)),
            pl.BlockSpec((T_ATT, 2 * D_HEAD), lambda b, h, t, qt, kt: (b * nq + kt[t], h)),
            pl.BlockSpec((T_ATT, D_VHEAD), lambda b, h, t, qt, kt: (b * nq + kt[t], h)),
            pl.BlockSpec((1, 2, SUB, SUB), lambda b, h, t, qt, kt: (h, 0, 0, 0)),
            vec, vec, vec, vec,
            pl.BlockSpec((1, D_VHEAD), lambda b, h, t, qt, kt: (0, 0)),
        ],
        out_specs=pl.BlockSpec((T_ATT, D_VHEAD), lambda b, h, t, qt, kt: (b * nq + qt[t], h)),
        scratch_shapes=[pltpu.VMEM((2, 1, T_ATT), F32),
                        pltpu.VMEM((2, 1, T_ATT), F32),
                        pltpu.VMEM((2, D_VHEAD, T_ATT), F32),
                        pltpu.VMEM((2, T_ATT, T_ATT), F32)],
    )
    return pl.pallas_call(
        _attn_prompt_kernel,
        grid_spec=grid_spec,
        out_shape=jax.ShapeDtypeStruct((batch * seq, N_HEADS * D_VHEAD), BF16),
        compiler_params=pltpu.CompilerParams(
            dimension_semantics=("parallel", "parallel", "arbitrary"),
            vmem_limit_bytes=VMEM_LIMIT),
        name="attn_prompt",
    )(qi_tab, ki_tab, q, k, v, dtiles, *lams, gh)


ROWS_S = 128
PAGES_PER_STEP = 4


def _attn_sample_kernel(pt_ref, q_ref, *refs, n_steps, t_new):
    pps = PAGES_PER_STEP
    page_refs = refs[:4 * pps]
    (knl_ref, knh_ref, vnl_ref, vnh_ref, bias_ref, bnew_ref,
     lq1_ref, lk1_ref, lq2_ref, lk2_ref, gh_ref,
     o_ref, qh_scr, m_scr, l_scr, acc_scr) = refs[4 * pps:]
    p = pl.program_id(1)
    rph = 2 * t_new
    nt = (((1,), (1,)), ((), ()))

    @pl.when(p == 0)
    def _():
        q = q_ref[0]
        zero = jnp.zeros((t_new, D_HEAD), F32)
        for h in range(N_HEADS):
            q1 = q[:, h * 2 * D_HEAD:h * 2 * D_HEAD + D_HEAD]
            q2 = q[:, h * 2 * D_HEAD + D_HEAD:(h + 1) * 2 * D_HEAD]
            qh_scr[h] = jnp.concatenate(
                [jnp.concatenate([q1, zero], axis=1), jnp.concatenate([zero, q2], axis=1)],
                axis=0).astype(BF16)
        m_scr[...] = jnp.full(m_scr.shape, -jnp.inf, F32)
        l_scr[...] = jnp.zeros(l_scr.shape, F32)
        acc_scr[...] = jnp.zeros(acc_scr.shape, F32)

    def head_rows(lo_ref, hi_ref, h, n):
        rows = pl.ds(h, n, stride=N_HEADS)
        return jnp.concatenate([lo_ref[rows, :], hi_ref[rows, :]], axis=1)

    def attend(kh, vh, bias):
        s = jnp.concatenate(
            [lax.dot_general(qh_scr[h], kh[h], nt, preferred_element_type=F32)
             for h in range(N_HEADS)], axis=0) + bias
        m_old = m_scr[...]
        m_new = jnp.maximum(m_old, jnp.max(s, axis=-1, keepdims=True))
        alpha = jnp.exp2(m_old - m_new)
        pr = jnp.exp2(s - m_new)
        l_scr[...] = alpha * l_scr[...] + jnp.sum(pr, axis=-1, keepdims=True)
        m_scr[...] = m_new
        prb = pr.astype(BF16)
        for h in range(N_HEADS):
            rs = slice(h * rph, (h + 1) * rph)
            pv = jnp.dot(prb[rs], vh[h], preferred_element_type=F32)
            acc_scr[h] = alpha[rs] * acc_scr[h] + pv

    kh, vh = [], []
    for h in range(N_HEADS):
        kh.append(jnp.concatenate(
            [head_rows(page_refs[4 * r], page_refs[4 * r + 1], h, PAGE) for r in range(pps)],
            axis=0).astype(BF16))
        vh.append(jnp.concatenate(
            [head_rows(page_refs[4 * r + 2], page_refs[4 * r + 3], h, PAGE) for r in range(pps)],
            axis=0).astype(BF16))
    attend(kh, vh, bias_ref[0])

    @pl.when(p == n_steps - 1)
    def _():
        pad = jnp.zeros((PAGE - t_new, D_VHEAD), F32)
        kn = [jnp.concatenate([head_rows(knl_ref, knh_ref, h, t_new), pad], axis=0).astype(BF16)
              for h in range(N_HEADS)]
        vn = [jnp.concatenate([head_rows(vnl_ref, vnh_ref, h, t_new), pad], axis=0).astype(BF16)
              for h in range(N_HEADS)]
        attend(kn, vn, bnew_ref[...])
        lam = _lambda_full(lq1_ref, lk1_ref, lq2_ref, lk2_ref)
        linv = 1.0 / l_scr[...]
        for h in range(N_HEADS):
            a = acc_scr[h] * linv[h * rph:(h + 1) * rph]
            o = a[:t_new] - lam * a[t_new:]
            o_ref[0, :, h * D_VHEAD:(h + 1) * D_VHEAD] = _head_norm(o, gh_ref[...])


def _attn_sample(page_table, q, cache_k, cache_v, k_new, v_new, bias, bias_new, lams, gh):
    n_seq, t_new, width = q.shape
    pps = PAGES_PER_STEP
    n_steps = page_table.shape[1] // pps
    assert n_steps * pps == page_table.shape[1]
    seq_blk = pl.BlockSpec((1, t_new, width), lambda s, p, pt: (s, 0, 0))
    vec = pl.BlockSpec((1, D_HEAD), lambda s, p, pt: (0, 0))

    def half_blk(rows, index, half):
        return pl.BlockSpec((rows * N_HEADS, D_HEAD), lambda s, p, pt: (index(s, p, pt), half))

    page_specs, page_args = [], []
    for r in range(pps):
        idx = lambda s, p, pt, r=r: pt[s, p * pps + r]
        for arr in (cache_k, cache_v):
            for half in range(2):
                page_specs.append(half_blk(PAGE, idx, half))
                page_args.append(arr)
    new_idx = lambda s, p, pt: s
    new_specs = [half_blk(t_new, new_idx, half) for _ in range(2) for half in range(2)]

    grid_spec = pltpu.PrefetchScalarGridSpec(
        num_scalar_prefetch=1,
        grid=(n_seq, n_steps),
        in_specs=[seq_blk] + page_specs + new_specs + [
            pl.BlockSpec((1, ROWS_S, pps * PAGE),
                         lambda s, p, pt: (jnp.where(p == n_steps - 1, 1, 0), 0, 0)),
            pl.BlockSpec((ROWS_S, PAGE), lambda s, p, pt: (0, 0)),
            vec, vec, vec, vec,
            pl.BlockSpec((1, D_VHEAD), lambda s, p, pt: (0, 0)),
        ],
        out_specs=seq_blk,
        scratch_shapes=[pltpu.VMEM((N_HEADS, 2 * t_new, 2 * D_HEAD), BF16),
                        pltpu.VMEM((ROWS_S, 1), F32),
                        pltpu.VMEM((ROWS_S, 1), F32),
                        pltpu.VMEM((N_HEADS, 2 * t_new, D_VHEAD), F32)],
    )
    return pl.pallas_call(
        functools.partial(_attn_sample_kernel, n_steps=n_steps, t_new=t_new),
        grid_spec=grid_spec,
        out_shape=jax.ShapeDtypeStruct((n_seq, t_new, width), F32),
        compiler_params=pltpu.CompilerParams(
            dimension_semantics=("parallel", "arbitrary"),
            vmem_limit_bytes=VMEM_LIMIT),
        name="attn_sample",
    )(page_table, q, *page_args, k_new, k_new, v_new, v_new, bias, bias_new, *lams, gh)


def _mix_kernel(sv_ref, u_ref, oap_ref, oas_ref, ga_ref, gm_ref, w_ref, b_ref, o_ref, *,
                prompt_chunks):
    is_prompt = pl.program_id(0) < prompt_chunks
    for g in range(GROUPS):
        cs = slice(g * GDIM, (g + 1) * GDIM)
        mixed = jnp.dot(w_ref[0, g], sv_ref[:, cs].astype(BF16),
                        preferred_element_type=F32) + b_ref[0, :, cs]
        oa = jnp.where(is_prompt, oap_ref[:, cs].astype(F32), oas_ref[:, cs])
        merged = (ga_ref[:, cs].astype(F32) * oa
                  + gm_ref[:, cs].astype(F32) * (u_ref[:, cs].astype(F32) * mixed))
        o_ref[:, cs] = merged.astype(BF16)


def _mix(sv, u, oa_prompt, oa_sample, gates, w_mix, b_mix):
    rows = sv.shape[0]
    prompt_chunks = oa_prompt.shape[0] // CHUNK
    blk = pl.BlockSpec((CHUNK, D_MODEL), lambda c: (c, 0))
    kind = lambda c: jnp.where(c >= prompt_chunks, 1, 0)
    return pl.pallas_call(
        functools.partial(_mix_kernel, prompt_chunks=prompt_chunks),
        grid=(rows // CHUNK,),
        in_specs=[
            blk, blk,
            pl.BlockSpec((CHUNK, D_MODEL), lambda c: (jnp.minimum(c, prompt_chunks - 1), 0)),
            pl.BlockSpec((CHUNK, D_MODEL), lambda c: (jnp.maximum(c - prompt_chunks, 0), 0)),
            pl.BlockSpec((CHUNK, D_MODEL), lambda c: (c, 0)),
            pl.BlockSpec((CHUNK, D_MODEL), lambda c: (c, 1)),
            pl.BlockSpec((1, GROUPS, CHUNK, CHUNK), lambda c: (kind(c), 0, 0, 0)),
            pl.BlockSpec((1, CHUNK, D_MODEL), lambda c: (kind(c), 0, 0)),
        ],
        out_specs=blk,
        out_shape=jax.ShapeDtypeStruct((rows, D_MODEL), BF16),
        compiler_params=pltpu.CompilerParams(
            dimension_semantics=("parallel",),
            vmem_limit_bytes=VMEM_LIMIT),
        name="gmlp_mix_merge",
    )(sv, u, oa_prompt, oa_sample, gates, gates, w_mix, b_mix)


def _out_proj_kernel(x_ref, w_ref, g_ref, h_ref, o_ref):
    w = w_ref[...]
    sub = TM_OUT // 2
    for r in range(2):
        rs = slice(r * sub, (r + 1) * sub)
        z = jnp.dot(x_ref[rs], w, preferred_element_type=F32)
        o_ref[rs] = h_ref[rs] + _rms_rows(z, g_ref[...])


def _out_proj(merged, w_o, g_post, h):
    rows = merged.shape[0]
    blk = pl.BlockSpec((TM_OUT, D_MODEL), lambda i: (i, 0))
    return pl.pallas_call(
        _out_proj_kernel,
        grid=(rows // TM_OUT,),
        in_specs=[blk,
                  pl.BlockSpec((D_MODEL, D_MODEL), lambda i: (0, 0)),
                  pl.BlockSpec((1, D_MODEL), lambda i: (0, 0)),
                  blk],
        out_specs=blk,
        out_shape=jax.ShapeDtypeStruct((rows, D_MODEL), F32),
        compiler_params=pltpu.CompilerParams(
            dimension_semantics=("parallel",),
            vmem_limit_bytes=VMEM_LIMIT),
        name="out_proj",
    )(merged, w_o, g_post, h)


def kernel(x_prompt, x_sample, cache_k, cache_v, page_table, rel_bias, ffn1_norm_pre, ffn1_w_gate, ffn1_w_up, ffn1_w_down, ffn1_norm_post, mix_norm_pre, w_in, lambda_q1, lambda_k1, lambda_q2, lambda_k2, attn_head_norm, gmlp_ln_g, gmlp_ln_b, gmlp_w_s, gmlp_b_s, w_o, mix_norm_post, ffn2_norm_pre, ffn2_w_gate, ffn2_w_up, ffn2_w_down, ffn2_norm_post):
    batch, seq, _ = x_prompt.shape
    n_seq, t_new, _ = x_sample.shape
    depth = cache_k.shape[0]
    assert depth == 1
    rows_p = batch * seq
    rows_s = n_seq * t_new
    rows = rows_p + rows_s
    assert t_new * 2 * N_HEADS == ROWS_S and CHUNK % t_new == 0

    row = lambda a: a.reshape(1, -1).astype(F32)
    bf = lambda a: a[0].astype(BF16)

    x = jnp.concatenate([x_prompt.reshape(rows_p, D_MODEL), x_sample.reshape(rows_s, D_MODEL)], axis=0)

    h1, n1 = _ffn(x, row(ffn1_norm_pre), bf(ffn1_w_gate), bf(ffn1_w_up), bf(ffn1_w_down),
                  row(ffn1_norm_post), row(mix_norm_pre), row0=0, rows=rows, emit_next=True)

    w_in_b = bf(w_in)
    (q_p,) = _proj(n1, w_in_b, sec0=0, nsec=1, row0=0, rows=rows_p, mode="q_bf16")
    (q_s,) = _proj(n1, w_in_b, sec0=0, nsec=1, row0=rows_p, rows=rows_s, mode="q_f32")
    k_p, k_pb = _proj(n1, w_in_b, sec0=1, nsec=1, row0=0, rows=rows_p, mode="kv")
    k_s, _ = _proj(n1, w_in_b, sec0=1, nsec=1, row0=rows_p, rows=rows_s, mode="kv")
    v_p, v_pb = _proj(n1, w_in_b, sec0=2, nsec=1, row0=0, rows=rows_p, mode="kv")
    v_s, _ = _proj(n1, w_in_b, sec0=2, nsec=1, row0=rows_p, rows=rows_s, mode="kv")
    (u,) = _proj(n1, w_in_b, sec0=3, nsec=1, row0=0, rows=rows, mode="gelu")
    (sv,) = _proj(n1, w_in_b, sec0=4, nsec=1, row0=0, rows=rows, mode="gelu_ln",
                  extra=(row(gmlp_ln_g), row(gmlp_ln_b)))
    (gates,) = _proj(n1, w_in_b, sec0=5, nsec=2, row0=0, rows=rows, mode="sigmoid")

    tbl = _bias_by_distance(rel_bias, SUB + 1)
    g = ((tbl - rel_bias[N_BUCKETS - 1][None, :].astype(F32)) * LOG2E).T
    vec = jnp.concatenate([g[:, ::-1], jnp.zeros((N_HEADS, SUB), F32)], axis=1)
    a = jnp.tile(vec, (1, SUB))[:, :SUB * 2 * SUB].reshape(N_HEADS, SUB, 2 * SUB)
    ii = jnp.arange(SUB)
    d1 = a[:, :, :SUB]
    d0 = jnp.where(ii[:, None] >= ii[None, :], a[:, :, SUB:], -jnp.inf)
    dtiles = jnp.stack([d0, d1], axis=1)

    expand = lambda b: jnp.broadcast_to(b[:, None], (N_HEADS, 2) + b.shape[1:]).reshape(ROWS_S, -1)
    b_last = expand(d1[:, :t_new, :])
    far = jnp.zeros((ROWS_S, (PAGES_PER_STEP - 1) * PAGE), F32)
    bias_s = jnp.stack([jnp.zeros((ROWS_S, PAGES_PER_STEP * PAGE), F32),
                        jnp.concatenate([far, b_last], axis=1)])
    b_new = jnp.concatenate([d0[:, :t_new, :t_new],
                             jnp.full((N_HEADS, t_new, PAGE - t_new), -jnp.inf, F32)], axis=2)
    bias_new = expand(b_new)

    lams = (row(lambda_q1), row(lambda_k1), row(lambda_q2), row(lambda_k2))
    gh = row(attn_head_norm)

    oa_p = _attn_prompt(q_p, k_pb, v_pb, jnp.swapaxes(dtiles, 2, 3), lams, gh, batch=batch, seq=seq)
    width = N_HEADS * D_VHEAD
    oa_s = _attn_sample(page_table,
                        q_s.reshape(n_seq, t_new, width),
                        cache_k.reshape(-1, D_VHEAD),
                        cache_v.reshape(-1, D_VHEAD),
                        k_s.reshape(-1, D_VHEAD), v_s.reshape(-1, D_VHEAD),
                        bias_s, bias_new, lams, gh)

    w_s = gmlp_w_s[0].astype(F32)
    w_prompt = jnp.tril(w_s)
    w_small = jnp.tril(w_s[:, :t_new, :t_new])
    eye = jnp.eye(CHUNK // t_new, dtype=F32)
    w_sample = jnp.einsum("ab,gts->gatbs", eye, w_small).reshape(GROUPS, CHUNK, CHUNK)
    w_mix = jnp.stack([w_prompt, w_sample]).astype(BF16)
    b_s = gmlp_b_s[0].astype(F32)
    b_prompt = jnp.repeat(b_s.T, GDIM, axis=1)
    b_sample = jnp.tile(jnp.repeat(b_s[:, :t_new].T, GDIM, axis=1), (CHUNK // t_new, 1))
    b_mix = jnp.stack([b_prompt, b_sample])

    merged = _mix(sv, u, oa_p, oa_s.reshape(rows_s, width), gates, w_mix, b_mix)
    h2 = _out_proj(merged, bf(w_o), row(mix_norm_post), h1)

    f2 = (row(ffn2_norm_pre), bf(ffn2_w_gate), bf(ffn2_w_up), bf(ffn2_w_down),
          row(ffn2_norm_post), row(ffn2_norm_post))
    (y_p,) = _ffn(h2, *f2, row0=0, rows=rows_p, emit_next=False)
    (y_s,) = _ffn(h2, *f2, row0=rows_p, rows=rows_s, emit_next=False)

    return (y_p.reshape(batch, seq, D_MODEL),
            y_s.reshape(n_seq, t_new, D_MODEL),
            k_p.reshape(1, batch, seq, N_HEADS, 2 * D_HEAD),
            v_p.reshape(1, batch, seq, N_HEADS, D_VHEAD),
            k_s.reshape(1, n_seq, t_new, N_HEADS, 2 * D_HEAD),
            v_s.reshape(1, n_seq, t_new, N_HEADS, D_VHEAD),
            sv[rows_p:].reshape(1, n_seq, t_new, D_MODEL))
```

```python
import functools
import math

import jax
import jax.numpy as jnp
from jax import lax
from jax.experimental import pallas as pl
from jax.experimental.pallas import tpu as pltpu

F32 = jnp.float32
BF16 = jnp.bfloat16

D_MODEL = 2048
D_FF = 5632
N_HEADS = 8
D_HEAD = 128
D_VHEAD = 256
SEC_W = 2048
N_BUCKETS = 32
MAX_DISTANCE = 128
PAGE = 128
CHUNK = 128
GROUPS = 16
GDIM = 128
EPS = 1e-6
SCALE = D_HEAD ** -0.5
LOG2E = math.log2(math.e)
QSCALE = SCALE * LOG2E
LAM_INIT = 0.8 - 0.6 * math.exp(-0.3 * 0)

VMEM_LIMIT = 56 * 1024 * 1024
VMEM_LIMIT_FFN = 60 * 1024 * 1024

TM_FFN = 1024
FFN_SPLIT = 2
TF_FFN = 512
TM_PROJ = 1024
PROJ_SPLIT = 4
TM_OUT = 512
T_ATT = 512


def _rms_rows(x, g):
    return x * lax.rsqrt(jnp.mean(x * x, axis=-1, keepdims=True) + EPS) * g


def _ffn_kernel(*refs, emit_next, n_first):
    if n_first is None:
        x_ref, *refs = refs
        x2_ref = None
    else:
        x_ref, x2_ref, *refs = refs
    gpre_ref, wg_ref, wu_ref, wd_ref, gpost_ref, gnext_ref, h_ref, xn_ref = refs[:8]
    n_ref = xn_ref if emit_next else None
    j = pl.program_id(1)

    def with_x(cond, fn):
        if x2_ref is None:
            pl.when(cond)(lambda: fn(x_ref))
        else:
            first = pl.program_id(0) < n_first
            pl.when(cond & first)(lambda: fn(x_ref))
            pl.when(cond & jnp.logical_not(first))(lambda: fn(x2_ref))

    def prologue(xr):
        xn_ref[...] = _rms_rows(xr[...], gpre_ref[...]).astype(BF16)
        h_ref[...] = jnp.zeros(h_ref.shape, F32)

    with_x(j == 0, prologue)

    wg, wu, wd = wg_ref[...], wu_ref[...], wd_ref[...]
    sub = TM_FFN // FFN_SPLIT
    for r in range(FFN_SPLIT):
        rs = slice(r * sub, (r + 1) * sub)
        xn = xn_ref[rs]
        g = jnp.dot(xn, wg, preferred_element_type=F32)
        u = jnp.dot(xn, wu, preferred_element_type=F32)
        a = (g * jax.nn.sigmoid(g) * u).astype(BF16)
        h_ref[rs] += jnp.dot(a, wd, preferred_element_type=F32)

    def epilogue(xr):
        h = xr[...] + 0.5 * _rms_rows(h_ref[...], gpost_ref[...])
        h_ref[...] = h
        if emit_next:
            n_ref[...] = _rms_rows(h, gnext_ref[...]).astype(BF16)

    with_x(j == pl.num_programs(1) - 1, epilogue)


def _ffn(x, gpre, wg, wu, wd, gpost, gnext, *, row0, rows, emit_next, x2=None):
    if x2 is not None:
        assert row0 == 0 and rows == x.shape[0] + x2.shape[0]
    nb = rows // TM_FFN
    b0 = row0 // TM_FFN
    nf = D_FF // TF_FFN
    n_first = None if x2 is None else x.shape[0] // TM_FFN
    vec = pl.BlockSpec((1, D_MODEL), lambda i, j: (0, 0))
    row_blk = pl.BlockSpec((TM_FFN, D_MODEL), lambda i, j: (i, 0))
    if x2 is None:
        x_specs = [pl.BlockSpec((TM_FFN, D_MODEL), lambda i, j: (i + b0, 0),
                                pipeline_mode=pl.Buffered(1))]
        xs = [x]
    else:
        x_specs = [pl.BlockSpec((TM_FFN, D_MODEL), lambda i, j: (jnp.minimum(i, n_first - 1), 0),
                                pipeline_mode=pl.Buffered(1)),
                   pl.BlockSpec((TM_FFN, D_MODEL), lambda i, j: (jnp.maximum(i - n_first, 0), 0),
                                pipeline_mode=pl.Buffered(1))]
        xs = [x, x2]
    out_shape = [jax.ShapeDtypeStruct((rows, D_MODEL), F32)]
    out_specs = [row_blk]
    scratch = []
    if emit_next:
        out_shape.append(jax.ShapeDtypeStruct((rows, D_MODEL), BF16))
        out_specs.append(row_blk)
    else:
        scratch.append(pltpu.VMEM((TM_FFN, D_MODEL), BF16))
    res = pl.pallas_call(
        functools.partial(_ffn_kernel, emit_next=emit_next, n_first=n_first),
        grid=(nb, nf),
        in_specs=x_specs + [
            vec,
            pl.BlockSpec((D_MODEL, TF_FFN), lambda i, j: (0, j)),
            pl.BlockSpec((D_MODEL, TF_FFN), lambda i, j: (0, j)),
            pl.BlockSpec((TF_FFN, D_MODEL), lambda i, j: (j, 0)),
            vec,
            vec,
        ],
        out_specs=out_specs,
        out_shape=out_shape,
        scratch_shapes=scratch,
        compiler_params=pltpu.CompilerParams(
            dimension_semantics=("parallel", "arbitrary"),
            vmem_limit_bytes=VMEM_LIMIT_FFN),
        name="ffn_emit_next" if emit_next else "ffn",
    )(*xs, gpre, wg, wu, wd, gpost, gnext)
    return res


def _gelu(x):
    return 0.5 * x * (1.0 + lax.erf(x * (1.0 / math.sqrt(2.0))))


def _proj_kernel(n_ref, w_ref, *rest, mode):
    w = w_ref[...]
    sub = TM_PROJ // PROJ_SPLIT
    for r in range(PROJ_SPLIT):
        rs = slice(r * sub, (r + 1) * sub)
        z = jnp.dot(n_ref[rs], w, preferred_element_type=F32)
        if mode == "q_bf16":
            (o_ref,) = rest
            o_ref[rs] = (z * QSCALE).astype(BF16)
        elif mode == "q_f32":
            (o_ref,) = rest
            o_ref[rs] = z * QSCALE
        elif mode == "kv":
            o_ref, ob_ref = rest
            for h in range(N_HEADS):
                o_ref[rs, h, :] = z[:, h * D_VHEAD:(h + 1) * D_VHEAD]
            ob_ref[rs] = z.astype(BF16)
        elif mode == "gelu":
            (o_ref,) = rest
            o_ref[rs] = _gelu(z).astype(BF16)
        elif mode == "gelu_ln":
            g_ref, b_ref, o_ref = rest
            a = _gelu(z)
            mu = jnp.mean(a, axis=-1, keepdims=True)
            ac = a - mu
            y = ac * lax.rsqrt(jnp.mean(ac * ac, axis=-1, keepdims=True) + EPS)
            o_ref[rs] = y * g_ref[...] + b_ref[...]
        elif mode == "sigmoid":
            (o_ref,) = rest
            o_ref[rs] = jax.nn.sigmoid(z).astype(BF16)
        else:
            raise ValueError(mode)


def _proj(n, w_in, *, sec0, nsec, row0, rows, mode, extra=()):
    nb = rows // TM_PROJ
    b0 = row0 // TM_PROJ
    blk = pl.BlockSpec((TM_PROJ, SEC_W), lambda s, i: (i, s))
    if mode in ("q_bf16", "gelu", "sigmoid"):
        out_dtypes = [BF16]
    elif mode == "kv":
        out_dtypes = [F32, BF16]
    else:
        out_dtypes = [F32]
    out_shape = [jax.ShapeDtypeStruct((rows, nsec * SEC_W), dt) for dt in out_dtypes]
    out_specs = [blk] * len(out_dtypes)
    if mode == "kv":
        assert nsec == 1
        out_shape[0] = jax.ShapeDtypeStruct((rows, N_HEADS, D_VHEAD), F32)
        out_specs[0] = pl.BlockSpec((TM_PROJ, N_HEADS, D_VHEAD), lambda s, i: (i, 0, 0))
    vec = pl.BlockSpec((1, SEC_W), lambda s, i: (0, 0))
    res = pl.pallas_call(
        functools.partial(_proj_kernel, mode=mode),
        grid=(nsec, nb),
        in_specs=[
            pl.BlockSpec((TM_PROJ, D_MODEL), lambda s, i: (i + b0, 0)),
            pl.BlockSpec((D_MODEL, SEC_W), lambda s, i: (0, s + sec0)),
        ] + [vec] * len(extra),
        out_specs=out_specs,
        out_shape=out_shape,
        compiler_params=pltpu.CompilerParams(
            dimension_semantics=("parallel", "parallel"),
            vmem_limit_bytes=VMEM_LIMIT),
        name="proj_" + mode,
    )(n, w_in, *extra)
    return res


def _lambda_full(lq1_ref, lk1_ref, lq2_ref, lk2_ref):
    s1 = jnp.sum(lq1_ref[...] * lk1_ref[...], axis=-1, keepdims=True)
    s2 = jnp.sum(lq2_ref[...] * lk2_ref[...], axis=-1, keepdims=True)
    return jnp.exp(s1) - jnp.exp(s2) + LAM_INIT


def _head_norm(o, gh):
    return _rms_rows(o, gh) * (1.0 - LAM_INIT)


def _bias_by_distance(rel_bias, nmax):
    n = jnp.arange(nmax)
    max_exact = N_BUCKETS // 2
    nf = jnp.maximum(n, 1).astype(F32)
    large = max_exact + (jnp.log(nf / max_exact) / math.log(MAX_DISTANCE / max_exact)
                         * (N_BUCKETS - max_exact)).astype(jnp.int32)
    large = jnp.minimum(large, N_BUCKETS - 1)
    bucket = jnp.where(n < max_exact, n, large)
    return rel_bias[bucket].astype(F32)


SUB = 128
Q_SPLIT = 2


def _attn_prompt_kernel(qi_tab, ki_tab, q_ref, k_ref, v_ref, dt_ref,
                        lq1_ref, lk1_ref, lq2_ref, lk2_ref, gh_ref,
                        o_ref, m_scr, l_scr, acc_scr, bias_scr):
    t = pl.program_id(2)
    qi = qi_tab[t]
    ki = ki_tab[t]
    d = qi - ki

    @pl.when(t == 0)
    def _():
        d0 = dt_ref[0, 0]
        d1 = dt_ref[0, 1]
        neg = jnp.full((SUB, SUB), -jnp.inf, F32)
        zero = jnp.zeros((SUB, SUB), F32)
        nsub = T_ATT // SUB
        for a in range(nsub):
            for b in range(nsub):
                ks, qs = slice(b * SUB, (b + 1) * SUB), slice(a * SUB, (a + 1) * SUB)
                bias_scr[0, ks, qs] = d0 if a == b else d1 if a == b + 1 else neg if a < b else zero
                bias_scr[1, ks, qs] = d1 if (a == 0 and b == nsub - 1) else zero

    @pl.when(ki == 0)
    def _():
        m_scr[...] = jnp.full(m_scr.shape, -jnp.inf, F32)
        l_scr[...] = jnp.zeros(l_scr.shape, F32)
        acc_scr[...] = jnp.zeros(acc_scr.shape, F32)

    k = k_ref[...]
    v = v_ref[...]
    nt = (((1,), (1,)), ((), ()))
    tn = (((0,), (0,)), ((), ()))
    qw = T_ATT // Q_SPLIT

    def update(mi, qh, with_bias):
        sl = slice(mi * D_HEAD, (mi + 1) * D_HEAD)
        qs = slice(qh * qw, (qh + 1) * qw)
        s = lax.dot_general(k[:, sl], q_ref[qs, sl], nt, preferred_element_type=F32)
        if with_bias:
            s = s + bias_scr[d, :, qs]
        m_old = m_scr[mi, :, qs]
        m_new = jnp.maximum(m_old, jnp.max(s, axis=0, keepdims=True))
        alpha = jnp.exp2(m_old - m_new)
        p = jnp.exp2(s - m_new)
        l_scr[mi, :, qs] = alpha * l_scr[mi, :, qs] + jnp.sum(p, axis=0, keepdims=True)
        acc_scr[mi, :, qs] = alpha * acc_scr[mi, :, qs] + lax.dot_general(
            v, p.astype(BF16), tn, preferred_element_type=F32)
        m_scr[mi, :, qs] = m_new

    @pl.when(d >= 2)
    def _():
        for qh in range(Q_SPLIT):
            for mi in range(2):
                update(mi, qh, False)

    @pl.when(d < 2)
    def _():
        for qh in range(Q_SPLIT):
            for mi in range(2):
                update(mi, qh, True)

    @pl.when(d == 0)
    def _():
        lam = _lambda_full(lq1_ref, lk1_ref, lq2_ref, lk2_ref)
        ot = acc_scr[0] / l_scr[0] - lam * (acc_scr[1] / l_scr[1])
        ot = ot * lax.rsqrt(jnp.mean(ot * ot, axis=0, keepdims=True) + EPS)
        o_ref[...] = (jnp.transpose(ot) * (gh_ref[...] * (1.0 - LAM_INIT))).astype(BF16)


def _attn_prompt(q, k, v, dtiles, lams, gh, *, batch, seq):
    nq = seq // T_ATT
    tri = [(qi, ki) for qi in range(nq) for ki in range(qi + 1)]
    qi_tab = jnp.asarray([a for a, _ in tri], jnp.int32)
    ki_tab = jnp.asarray([b for _, b in tri], jnp.int32)
    vec = pl.BlockSpec((1, D_HEAD), lambda b, h, t, qt, kt: (0, 0))
    grid_spec = pltpu.PrefetchScalarGridSpec(
        num_scalar_prefetch=2,
        grid=(batch, N_HEADS, len(tri)),
        in_specs=[
            pl.BlockSpec((T_ATT, 2 * D_HEAD), lambda b, h, t, qt, kt: (b * nq + qt[t], h)),
            pl.BlockSpec((T_ATT, 2 * D_HEAD), lambda b, h, t, qt, kt: (b * nq + kt[t], h)),
            pl.BlockSpec((T_ATT, D_VHEAD), lambda b, h, t, qt, kt: (b * nq + kt[t], h)),
            pl.BlockSpec((1, 2, SUB, SUB), lambda b, h, t, qt, kt: (h, 0, 0, 0)),
            vec, vec, vec, vec,
            pl.BlockSpec((1, D_VHEAD), lambda b, h, t, qt, kt: (0, 0)),
        ],
        out_specs=pl.BlockSpec((T_ATT, D_VHEAD), lambda b, h, t, qt, kt: (b * nq + qt[t], h)),
        scratch_shapes=[pltpu.VMEM((2, 1, T_ATT), F32),
                        pltpu.VMEM((2, 1, T_ATT), F32),
                        pltpu.VMEM((2, D_VHEAD, T_ATT), F32),
                        pltpu.VMEM((2, T_ATT, T_ATT), F32)],
    )
    return pl.pallas_call(
        _attn_prompt_kernel,
        grid_spec=grid_spec,
        out_shape=jax.ShapeDtypeStruct((batch * seq, N_HEADS * D_VHEAD), BF16),
        compiler_params=pltpu.CompilerParams(
            dimension_semantics=("parallel", "parallel", "arbitrary"),
            vmem_limit_bytes=VMEM_LIMIT),
        name="attn_prompt",
    )(qi_tab, ki_tab, q, k, v, dtiles, *lams, gh)


ROWS_S = 128
PAGES_PER_STEP = 8


def _attn_sample_kernel(pt_ref, q_ref, *refs, n_steps, t_new):
    pps = PAGES_PER_STEP
    page_refs = refs[:4 * pps]
    (knl_ref, knh_ref, vnl_ref, vnh_ref, bias_ref, bnew_ref,
     lq1_ref, lk1_ref, lq2_ref, lk2_ref, gh_ref,
     o_ref, qh_scr, m_scr, l_scr, acc_scr) = refs[4 * pps:]
    p = pl.program_id(1)
    rph = 2 * t_new
    nt = (((1,), (1,)), ((), ()))

    @pl.when(p == 0)
    def _():
        q = q_ref[0]
        zero = jnp.zeros((t_new, D_HEAD), F32)
        for h in range(N_HEADS):
            q1 = q[:, h * 2 * D_HEAD:h * 2 * D_HEAD + D_HEAD]
            q2 = q[:, h * 2 * D_HEAD + D_HEAD:(h + 1) * 2 * D_HEAD]
            qh_scr[h] = jnp.concatenate(
                [jnp.concatenate([q1, zero], axis=1), jnp.concatenate([zero, q2], axis=1)],
                axis=0).astype(BF16)
        m_scr[...] = jnp.full(m_scr.shape, -jnp.inf, F32)
        l_scr[...] = jnp.zeros(l_scr.shape, F32)
        acc_scr[...] = jnp.zeros(acc_scr.shape, F32)

    def head_rows(lo_ref, hi_ref, h, n):
        rows = pl.ds(h, n, stride=N_HEADS)
        return jnp.concatenate([lo_ref[rows, :], hi_ref[rows, :]], axis=1)

    def attend(kh, vh, bias):
        s = jnp.concatenate(
            [lax.dot_general(qh_scr[h], kh[h], nt, preferred_element_type=F32)
             for h in range(N_HEADS)], axis=0) + bias
        m_old = m_scr[...]
        m_new = jnp.maximum(m_old, jnp.max(s, axis=-1, keepdims=True))
        alpha = jnp.exp2(m_old - m_new)
        pr = jnp.exp2(s - m_new)
        l_scr[...] = alpha * l_scr[...] + jnp.sum(pr, axis=-1, keepdims=True)
        m_scr[...] = m_new
        prb = pr.astype(BF16)
        for h in range(N_HEADS):
            rs = slice(h * rph, (h + 1) * rph)
            pv = jnp.dot(prb[rs], vh[h], preferred_element_type=F32)
            acc_scr[h] = alpha[rs] * acc_scr[h] + pv

    kh, vh = [], []
    for h in range(N_HEADS):
        kh.append(jnp.concatenate(
            [head_rows(page_refs[4 * r], page_refs[4 * r + 1], h, PAGE) for r in range(pps)],
            axis=0).astype(BF16))
        vh.append(jnp.concatenate(
            [head_rows(page_refs[4 * r + 2], page_refs[4 * r + 3], h, PAGE) for r in range(pps)],
            axis=0).astype(BF16))
    attend(kh, vh, bias_ref[0])

    @pl.when(p == n_steps - 1)
    def _():
        pad = jnp.zeros((PAGE - t_new, D_VHEAD), F32)
        kn = [jnp.concatenate([head_rows(knl_ref, knh_ref, h, t_new), pad], axis=0).astype(BF16)
              for h in range(N_HEADS)]
        vn = [jnp.concatenate([head_rows(vnl_ref, vnh_ref, h, t_new), pad], axis=0).astype(BF16)
              for h in range(N_HEADS)]
        attend(kn, vn, bnew_ref[...])
        lam = _lambda_full(lq1_ref, lk1_ref, lq2_ref, lk2_ref)
        linv = 1.0 / l_scr[...]
        for h in range(N_HEADS):
            a = acc_scr[h] * linv[h * rph:(h + 1) * rph]
            o = a[:t_new] - lam * a[t_new:]
            o_ref[0, :, h * D_VHEAD:(h + 1) * D_VHEAD] = _head_norm(o, gh_ref[...])


def _attn_sample(page_table, q, cache_k, cache_v, k_new, v_new, bias, bias_new, lams, gh):
    n_seq, t_new, width = q.shape
    pps = PAGES_PER_STEP
    n_steps = page_table.shape[1] // pps
    assert n_steps * pps == page_table.shape[1]
    seq_blk = pl.BlockSpec((1, t_new, width), lambda s, p, pt: (s, 0, 0))
    vec = pl.BlockSpec((1, D_HEAD), lambda s, p, pt: (0, 0))

    def half_blk(rows, index, half):
        return pl.BlockSpec((rows * N_HEADS, D_HEAD), lambda s, p, pt: (index(s, p, pt), half))

    page_specs, page_args = [], []
    for r in range(pps):
        idx = lambda s, p, pt, r=r: pt[s, p * pps + r]
        for arr in (cache_k, cache_v):
            for half in range(2):
                page_specs.append(half_blk(PAGE, idx, half))
                page_args.append(arr)
    new_idx = lambda s, p, pt: s
    new_specs = [half_blk(t_new, new_idx, half) for _ in range(2) for half in range(2)]

    grid_spec = pltpu.PrefetchScalarGridSpec(
        num_scalar_prefetch=1,
        grid=(n_seq, n_steps),
        in_specs=[seq_blk] + page_specs + new_specs + [
            pl.BlockSpec((1, ROWS_S, pps * PAGE),
                         lambda s, p, pt: (jnp.where(p == n_steps - 1, 1, 0), 0, 0)),
            pl.BlockSpec((ROWS_S, PAGE), lambda s, p, pt: (0, 0)),
            vec, vec, vec, vec,
            pl.BlockSpec((1, D_VHEAD), lambda s, p, pt: (0, 0)),
        ],
        out_specs=seq_blk,
        scratch_shapes=[pltpu.VMEM((N_HEADS, 2 * t_new, 2 * D_HEAD), BF16),
                        pltpu.VMEM((ROWS_S, 1), F32),
                        pltpu.VMEM((ROWS_S, 1), F32),
                        pltpu.VMEM((N_HEADS, 2 * t_new, D_VHEAD), F32)],
    )
    return pl.pallas_call(
        functools.partial(_attn_sample_kernel, n_steps=n_steps, t_new=t_new),
        grid_spec=grid_spec,
        out_shape=jax.ShapeDtypeStruct((n_seq, t_new, width), F32),
        compiler_params=pltpu.CompilerParams(
            dimension_semantics=("parallel", "arbitrary"),
            vmem_limit_bytes=VMEM_LIMIT),
        name="attn_sample",
    )(page_table, q, *page_args, k_new, k_new, v_new, v_new, bias, bias_new, *lams, gh)


def _mix_kernel(sv_ref, u_ref, oap_ref, oas_ref, ga_ref, gm_ref, w_ref, b_ref, o_ref, *,
                prompt_chunks):
    is_prompt = pl.program_id(0) < prompt_chunks
    for g in range(GROUPS):
        cs = slice(g * GDIM, (g + 1) * GDIM)
        mixed = jnp.dot(w_ref[0, g], sv_ref[:, cs].astype(BF16),
                        preferred_element_type=F32) + b_ref[0, :, cs]
        oa = jnp.where(is_prompt, oap_ref[:, cs].astype(F32), oas_ref[:, cs])
        merged = (ga_ref[:, cs].astype(F32) * oa
                  + gm_ref[:, cs].astype(F32) * (u_ref[:, cs].astype(F32) * mixed))
        o_ref[:, cs] = merged.astype(BF16)


def _mix(sv, u, oa_prompt, oa_sample, gates, w_mix, b_mix):
    rows = sv.shape[0]
    prompt_chunks = oa_prompt.shape[0] // CHUNK
    blk = pl.BlockSpec((CHUNK, D_MODEL), lambda c: (c, 0))
    kind = lambda c: jnp.where(c >= prompt_chunks, 1, 0)
    return pl.pallas_call(
        functools.partial(_mix_kernel, prompt_chunks=prompt_chunks),
        grid=(rows // CHUNK,),
        in_specs=[
            blk, blk,
            pl.BlockSpec((CHUNK, D_MODEL), lambda c: (jnp.minimum(c, prompt_chunks - 1), 0)),
            pl.BlockSpec((CHUNK, D_MODEL), lambda c: (jnp.maximum(c - prompt_chunks, 0), 0)),
            pl.BlockSpec((CHUNK, D_MODEL), lambda c: (c, 0)),
            pl.BlockSpec((CHUNK, D_MODEL), lambda c: (c, 1)),
            pl.BlockSpec((1, GROUPS, CHUNK, CHUNK), lambda c: (kind(c), 0, 0, 0)),
            pl.BlockSpec((1, CHUNK, D_MODEL), lambda c: (kind(c), 0, 0)),
        ],
        out_specs=blk,
        out_shape=jax.ShapeDtypeStruct((rows, D_MODEL), BF16),
        compiler_params=pltpu.CompilerParams(
            dimension_semantics=("parallel",),
            vmem_limit_bytes=VMEM_LIMIT),
        name="gmlp_mix_merge",
    )(sv, u, oa_prompt, oa_sample, gates, gates, w_mix, b_mix)


def _out_proj_kernel(x_ref, w_ref, g_ref, h_ref, o_ref):
    w = w_ref[...]
    sub = TM_OUT // 2
    for r in range(2):
        rs = slice(r * sub, (r + 1) * sub)
        z = jnp.dot(x_ref[rs], w, preferred_element_type=F32)
        o_ref[rs] = h_ref[rs] + _rms_rows(z, g_ref[...])


def _out_proj(merged, w_o, g_post, h):
    rows = merged.shape[0]
    blk = pl.BlockSpec((TM_OUT, D_MODEL), lambda i: (i, 0))
    return pl.pallas_call(
        _out_proj_kernel,
        grid=(rows // TM_OUT,),
        in_specs=[blk,
                  pl.BlockSpec((D_MODEL, D_MODEL), lambda i: (0, 0)),
                  pl.BlockSpec((1, D_MODEL), lambda i: (0, 0)),
                  blk],
        out_specs=blk,
        out_shape=jax.ShapeDtypeStruct((rows, D_MODEL), F32),
        compiler_params=pltpu.CompilerParams(
            dimension_semantics=("parallel",),
            vmem_limit_bytes=VMEM_LIMIT),
        name="out_proj",
    )(merged, w_o, g_post, h)


def kernel(x_prompt, x_sample, cache_k, cache_v, page_table, rel_bias, ffn1_norm_pre, ffn1_w_gate, ffn1_w_up, ffn1_w_down, ffn1_norm_post, mix_norm_pre, w_in, lambda_q1, lambda_k1, lambda_q2, lambda_k2, attn_head_norm, gmlp_ln_g, gmlp_ln_b, gmlp_w_s, gmlp_b_s, w_o, mix_norm_post, ffn2_norm_pre, ffn2_w_gate, ffn2_w_up, ffn2_w_down, ffn2_norm_post):
    batch, seq, _ = x_prompt.shape
    n_seq, t_new, _ = x_sample.shape
    depth = cache_k.shape[0]
    assert depth == 1
    rows_p = batch * seq
    rows_s = n_seq * t_new
    rows = rows_p + rows_s
    assert t_new * 2 * N_HEADS == ROWS_S and CHUNK % t_new == 0

    row = lambda a: a.reshape(1, -1).astype(F32)
    bf = lambda a: a[0].astype(BF16)

    h1, n1 = _ffn(x_prompt.reshape(rows_p, D_MODEL), row(ffn1_norm_pre), bf(ffn1_w_gate),
                  bf(ffn1_w_up), bf(ffn1_w_down), row(ffn1_norm_post), row(mix_norm_pre),
                  row0=0, rows=rows, emit_next=True, x2=x_sample.reshape(rows_s, D_MODEL))

    w_in_b = bf(w_in)
    (q_p,) = _proj(n1, w_in_b, sec0=0, nsec=1, row0=0, rows=rows_p, mode="q_bf16")
    (q_s,) = _proj(n1, w_in_b, sec0=0, nsec=1, row0=rows_p, rows=rows_s, mode="q_f32")
    k_p, k_pb = _proj(n1, w_in_b, sec0=1, nsec=1, row0=0, rows=rows_p, mode="kv")
    k_s, _ = _proj(n1, w_in_b, sec0=1, nsec=1, row0=rows_p, rows=rows_s, mode="kv")
    v_p, v_pb = _proj(n1, w_in_b, sec0=2, nsec=1, row0=0, rows=rows_p, mode="kv")
    v_s, _ = _proj(n1, w_in_b, sec0=2, nsec=1, row0=rows_p, rows=rows_s, mode="kv")
    (u,) = _proj(n1, w_in_b, sec0=3, nsec=1, row0=0, rows=rows, mode="gelu")
    (sv,) = _proj(n1, w_in_b, sec0=4, nsec=1, row0=0, rows=rows, mode="gelu_ln",
                  extra=(row(gmlp_ln_g), row(gmlp_ln_b)))
    (gates,) = _proj(n1, w_in_b, sec0=5, nsec=2, row0=0, rows=rows, mode="sigmoid")

    tbl = _bias_by_distance(rel_bias, SUB + 1)
    g = ((tbl - rel_bias[N_BUCKETS - 1][None, :].astype(F32)) * LOG2E).T
    vec = jnp.concatenate([g[:, ::-1], jnp.zeros((N_HEADS, SUB), F32)], axis=1)
    a = jnp.tile(vec, (1, SUB))[:, :SUB * 2 * SUB].reshape(N_HEADS, SUB, 2 * SUB)
    ii = jnp.arange(SUB)
    d1 = a[:, :, :SUB]
    d0 = jnp.where(ii[:, None] >= ii[None, :], a[:, :, SUB:], -jnp.inf)
    dtiles = jnp.stack([d0, d1], axis=1)

    expand = lambda b: jnp.broadcast_to(b[:, None], (N_HEADS, 2) + b.shape[1:]).reshape(ROWS_S, -1)
    b_last = expand(d1[:, :t_new, :])
    far = jnp.zeros((ROWS_S, (PAGES_PER_STEP - 1) * PAGE), F32)
    bias_s = jnp.stack([jnp.zeros((ROWS_S, PAGES_PER_STEP * PAGE), F32),
                        jnp.concatenate([far, b_last], axis=1)])
    b_new = jnp.concatenate([d0[:, :t_new, :t_new],
                             jnp.full((N_HEADS, t_new, PAGE - t_new), -jnp.inf, F32)], axis=2)
    bias_new = expand(b_new)

    lams = (row(lambda_q1), row(lambda_k1), row(lambda_q2), row(lambda_k2))
    gh = row(attn_head_norm)

    oa_p = _attn_prompt(q_p, k_pb, v_pb, jnp.swapaxes(dtiles, 2, 3), lams, gh, batch=batch, seq=seq)
    width = N_HEADS * D_VHEAD
    oa_s = _attn_sample(page_table,
                        q_s.reshape(n_seq, t_new, width),
                        cache_k.reshape(-1, D_VHEAD),
                        cache_v.reshape(-1, D_VHEAD),
                        k_s.reshape(-1, D_VHEAD), v_s.reshape(-1, D_VHEAD),
                        bias_s, bias_new, lams, gh)

    w_s = gmlp_w_s[0].astype(F32)
    w_prompt = jnp.tril(w_s)
    w_small = jnp.tril(w_s[:, :t_new, :t_new])
    eye = jnp.eye(CHUNK // t_new, dtype=F32)
    w_sample = jnp.einsum("ab,gts->gatbs", eye, w_small).reshape(GROUPS, CHUNK, CHUNK)
    w_mix = jnp.stack([w_prompt, w_sample]).astype(BF16)
    b_s = gmlp_b_s[0].astype(F32)
    b_prompt = jnp.repeat(b_s.T, GDIM, axis=1)
    b_sample = jnp.tile(jnp.repeat(b_s[:, :t_new].T, GDIM, axis=1), (CHUNK // t_new, 1))
    b_mix = jnp.stack([b_prompt, b_sample])

    merged = _mix(sv, u, oa_p, oa_s.reshape(rows_s, width), gates, w_mix, b_mix)
    h2 = _out_proj(merged, bf(w_o), row(mix_norm_post), h1)

    f2 = (row(ffn2_norm_pre), bf(ffn2_w_gate), bf(ffn2_w_up), bf(ffn2_w_down),
          row(ffn2_norm_post), row(ffn2_norm_post))
    (y_p,) = _ffn(h2, *f2, row0=0, rows=rows_p, emit_next=False)
    (y_s,) = _ffn(h2, *f2, row0=rows_p, rows=rows_s, emit_next=False)

    return (y_p.reshape(batch, seq, D_MODEL),
            y_s.reshape(n_seq, t_new, D_MODEL),
            k_p.reshape(1, batch, seq, N_HEADS, 2 * D_HEAD),
            v_p.reshape(1, batch, seq, N_HEADS, D_VHEAD),
            k_s.reshape(1, n_seq, t_new, N_HEADS, 2 * D_HEAD),
            v_s.reshape(1, n_seq, t_new, N_HEADS, D_VHEAD),
            sv[rows_p:].reshape(1, n_seq, t_new, D_MODEL))
```

```python
import functools
import math

import jax
import jax.numpy as jnp
from jax import lax
from jax.experimental import pallas as pl
from jax.experimental.pallas import tpu as pltpu

F32 = jnp.float32
BF16 = jnp.bfloat16

D_MODEL = 2048
D_FF = 5632
N_HEADS = 8
D_HEAD = 128
D_VHEAD = 256
SEC_W = 2048
N_BUCKETS = 32
MAX_DISTANCE = 128
PAGE = 128
CHUNK = 128
GROUPS = 16
GDIM = 128
EPS = 1e-6
SCALE = D_HEAD ** -0.5
LOG2E = math.log2(math.e)
QSCALE = SCALE * LOG2E
LAM_INIT = 0.8 - 0.6 * math.exp(-0.3 * 0)

VMEM_LIMIT = 56 * 1024 * 1024
VMEM_LIMIT_FFN = 60 * 1024 * 1024

TM_FFN = 1024
FFN_SPLIT = 2
TF_FFN = 512
TM_PROJ = 1024
PROJ_SPLIT = 4
TM_OUT = 512
T_ATT = 512


def _rms_rows(x, g):
    return x * lax.rsqrt(jnp.mean(x * x, axis=-1, keepdims=True) + EPS) * g


def _ffn_kernel(*refs, emit_next, n_first):
    if n_first is None:
        x_ref, *refs = refs
        x2_ref = None
    else:
        x_ref, x2_ref, *refs = refs
    gpre_ref, wg_ref, wu_ref, wd_ref, gpost_ref, gnext_ref, h_ref, xn_ref = refs[:8]
    n_ref = xn_ref if emit_next else None
    j = pl.program_id(1)

    def with_x(cond, fn):
        if x2_ref is None:
            pl.when(cond)(lambda: fn(x_ref))
        else:
            first = pl.program_id(0) < n_first
            pl.when(cond & first)(lambda: fn(x_ref))
            pl.when(cond & jnp.logical_not(first))(lambda: fn(x2_ref))

    def prologue(xr):
        xn_ref[...] = _rms_rows(xr[...], gpre_ref[...]).astype(BF16)
        h_ref[...] = jnp.zeros(h_ref.shape, F32)

    with_x(j == 0, prologue)

    wg, wu, wd = wg_ref[...], wu_ref[...], wd_ref[...]
    sub = TM_FFN // FFN_SPLIT
    for r in range(FFN_SPLIT):
        rs = slice(r * sub, (r + 1) * sub)
        xn = xn_ref[rs]
        g = jnp.dot(xn, wg, preferred_element_type=F32)
        u = jnp.dot(xn, wu, preferred_element_type=F32)
        a = (g * jax.nn.sigmoid(g) * u).astype(BF16)
        h_ref[rs] += jnp.dot(a, wd, preferred_element_type=F32)

    def epilogue(xr):
        h = xr[...] + 0.5 * _rms_rows(h_ref[...], gpost_ref[...])
        h_ref[...] = h
        if emit_next:
            n_ref[...] = _rms_rows(h, gnext_ref[...]).astype(BF16)

    with_x(j == pl.num_programs(1) - 1, epilogue)


def _ffn(x, gpre, wg, wu, wd, gpost, gnext, *, row0, rows, emit_next, x2=None):
    if x2 is not None:
        assert row0 == 0 and rows == x.shape[0] + x2.shape[0]
    nb = rows // TM_FFN
    b0 = row0 // TM_FFN
    nf = D_FF // TF_FFN
    n_first = None if x2 is None else x.shape[0] // TM_FFN
    vec = pl.BlockSpec((1, D_MODEL), lambda i, j: (0, 0))
    row_blk = pl.BlockSpec((TM_FFN, D_MODEL), lambda i, j: (i, 0))
    if x2 is None:
        x_specs = [pl.BlockSpec((TM_FFN, D_MODEL), lambda i, j: (i + b0, 0),
                                pipeline_mode=pl.Buffered(1))]
        xs = [x]
    else:
        x_specs = [pl.BlockSpec((TM_FFN, D_MODEL), lambda i, j: (jnp.minimum(i, n_first - 1), 0),
                                pipeline_mode=pl.Buffered(1)),
                   pl.BlockSpec((TM_FFN, D_MODEL), lambda i, j: (jnp.maximum(i - n_first, 0), 0),
                                pipeline_mode=pl.Buffered(1))]
        xs = [x, x2]
    out_shape = [jax.ShapeDtypeStruct((rows, D_MODEL), F32)]
    out_specs = [row_blk]
    scratch = []
    if emit_next:
        out_shape.append(jax.ShapeDtypeStruct((rows, D_MODEL), BF16))
        out_specs.append(row_blk)
    else:
        scratch.append(pltpu.VMEM((TM_FFN, D_MODEL), BF16))
    res = pl.pallas_call(
        functools.partial(_ffn_kernel, emit_next=emit_next, n_first=n_first),
        grid=(nb, nf),
        in_specs=x_specs + [
            vec,
            pl.BlockSpec((D_MODEL, TF_FFN), lambda i, j: (0, j)),
            pl.BlockSpec((D_MODEL, TF_FFN), lambda i, j: (0, j)),
            pl.BlockSpec((TF_FFN, D_MODEL), lambda i, j: (j, 0)),
            vec,
            vec,
        ],
        out_specs=out_specs,
        out_shape=out_shape,
        scratch_shapes=scratch,
        compiler_params=pltpu.CompilerParams(
            dimension_semantics=("parallel", "arbitrary"),
            vmem_limit_bytes=VMEM_LIMIT_FFN),
        name="ffn_emit_next" if emit_next else "ffn",
    )(*xs, gpre, wg, wu, wd, gpost, gnext)
    return res


def _gelu(x):
    return 0.5 * x * (1.0 + lax.erf(x * (1.0 / math.sqrt(2.0))))


def _proj_kernel(n_ref, w_ref, *rest, mode):
    w = w_ref[...]
    sub = TM_PROJ // PROJ_SPLIT
    for r in range(PROJ_SPLIT):
        rs = slice(r * sub, (r + 1) * sub)
        z = jnp.dot(n_ref[rs], w, preferred_element_type=F32)
        if mode == "q_bf16":
            (o_ref,) = rest
            o_ref[rs] = (z * QSCALE).astype(BF16)
        elif mode == "q_f32":
            (o_ref,) = rest
            o_ref[rs] = z * QSCALE
        elif mode == "kv":
            o_ref, ob_ref = rest
            for h in range(N_HEADS):
                o_ref[rs, h, :] = z[:, h * D_VHEAD:(h + 1) * D_VHEAD]
            ob_ref[rs] = z.astype(BF16)
        elif mode == "gelu":
            (o_ref,) = rest
            o_ref[rs] = _gelu(z).astype(BF16)
        elif mode == "gelu_ln":
            g_ref, b_ref, o_ref = rest
            a = _gelu(z)
            mu = jnp.mean(a, axis=-1, keepdims=True)
            ac = a - mu
            y = ac * lax.rsqrt(jnp.mean(ac * ac, axis=-1, keepdims=True) + EPS)
            o_ref[rs] = y * g_ref[...] + b_ref[...]
        elif mode == "sigmoid":
            (o_ref,) = rest
            o_ref[rs] = jax.nn.sigmoid(z).astype(BF16)
        else:
            raise ValueError(mode)


def _proj(n, w_in, *, sec0, nsec, row0, rows, mode, extra=()):
    nb = rows // TM_PROJ
    b0 = row0 // TM_PROJ
    blk = pl.BlockSpec((TM_PROJ, SEC_W), lambda s, i: (i, s))
    if mode in ("q_bf16", "gelu", "sigmoid"):
        out_dtypes = [BF16]
    elif mode == "kv":
        out_dtypes = [F32, BF16]
    else:
        out_dtypes = [F32]
    out_shape = [jax.ShapeDtypeStruct((rows, nsec * SEC_W), dt) for dt in out_dtypes]
    out_specs = [blk] * len(out_dtypes)
    if mode == "kv":
        assert nsec == 1
        out_shape[0] = jax.ShapeDtypeStruct((rows, N_HEADS, D_VHEAD), F32)
        out_specs[0] = pl.BlockSpec((TM_PROJ, N_HEADS, D_VHEAD), lambda s, i: (i, 0, 0))
    vec = pl.BlockSpec((1, SEC_W), lambda s, i: (0, 0))
    res = pl.pallas_call(
        functools.partial(_proj_kernel, mode=mode),
        grid=(nsec, nb),
        in_specs=[
            pl.BlockSpec((TM_PROJ, D_MODEL), lambda s, i: (i + b0, 0)),
            pl.BlockSpec((D_MODEL, SEC_W), lambda s, i: (0, s + sec0)),
        ] + [vec] * len(extra),
        out_specs=out_specs,
        out_shape=out_shape,
        compiler_params=pltpu.CompilerParams(
            dimension_semantics=("parallel", "parallel"),
            vmem_limit_bytes=VMEM_LIMIT),
        name="proj_" + mode,
    )(n, w_in, *extra)
    return res


def _lambda_full(lq1_ref, lk1_ref, lq2_ref, lk2_ref):
    s1 = jnp.sum(lq1_ref[...] * lk1_ref[...], axis=-1, keepdims=True)
    s2 = jnp.sum(lq2_ref[...] * lk2_ref[...], axis=-1, keepdims=True)
    return jnp.exp(s1) - jnp.exp(s2) + LAM_INIT


def _head_norm(o, gh):
    return _rms_rows(o, gh) * (1.0 - LAM_INIT)


def _bias_by_distance(rel_bias, nmax):
    n = jnp.arange(nmax)
    max_exact = N_BUCKETS // 2
    nf = jnp.maximum(n, 1).astype(F32)
    large = max_exact + (jnp.log(nf / max_exact) / math.log(MAX_DISTANCE / max_exact)
                         * (N_BUCKETS - max_exact)).astype(jnp.int32)
    large = jnp.minimum(large, N_BUCKETS - 1)
    bucket = jnp.where(n < max_exact, n, large)
    return rel_bias[bucket].astype(F32)


SUB = 128
Q_SPLIT = 2
HEADS_PER_STEP = 2


def _attn_prompt_kernel(qi_tab, ki_tab, q_ref, k_ref, v_ref, dt_ref,
                        lq1_ref, lk1_ref, lq2_ref, lk2_ref, gh_ref,
                        o_ref, m_scr, l_scr, acc_scr, bias_scr):
    t = pl.program_id(2)
    qi = qi_tab[t]
    ki = ki_tab[t]
    d = qi - ki

    @pl.when(t == 0)
    def _():
        neg = jnp.full((SUB, SUB), -jnp.inf, F32)
        zero = jnp.zeros((SUB, SUB), F32)
        nsub = T_ATT // SUB
        for hh in range(HEADS_PER_STEP):
            d0 = dt_ref[hh, 0]
            d1 = dt_ref[hh, 1]
            for a in range(nsub):
                for b in range(nsub):
                    ks, qs = slice(b * SUB, (b + 1) * SUB), slice(a * SUB, (a + 1) * SUB)
                    bias_scr[hh, 0, ks, qs] = (d0 if a == b else d1 if a == b + 1
                                               else neg if a < b else zero)
                    bias_scr[hh, 1, ks, qs] = d1 if (a == 0 and b == nsub - 1) else zero

    @pl.when(ki == 0)
    def _():
        m_scr[...] = jnp.full(m_scr.shape, -jnp.inf, F32)
        l_scr[...] = jnp.zeros(l_scr.shape, F32)
        acc_scr[...] = jnp.zeros(acc_scr.shape, F32)

    nt = (((1,), (1,)), ((), ()))
    tn = (((0,), (0,)), ((), ()))
    qw = T_ATT // Q_SPLIT

    def update(hh, mi, qh, near):
        c0 = hh * 2 * D_HEAD + mi * D_HEAD
        qs = slice(qh * qw, (qh + 1) * qw)
        kr = slice(0, (qh + 1) * qw if near == 0 else T_ATT)
        s = lax.dot_general(k_ref[kr, c0:c0 + D_HEAD], q_ref[qs, c0:c0 + D_HEAD], nt,
                            preferred_element_type=F32)
        if near is not None:
            s = s + bias_scr[hh, near, kr, qs]
        m_old = m_scr[hh, mi, :, qs]
        m_new = jnp.maximum(m_old, jnp.max(s, axis=0, keepdims=True))
        alpha = jnp.exp2(m_old - m_new)
        p = jnp.exp2(s - m_new)
        l_scr[hh, mi, :, qs] = alpha * l_scr[hh, mi, :, qs] + jnp.sum(p, axis=0, keepdims=True)
        acc_scr[hh, mi, :, qs] = alpha * acc_scr[hh, mi, :, qs] + lax.dot_general(
            v_ref[kr, hh * D_VHEAD:(hh + 1) * D_VHEAD], p.astype(BF16), tn,
            preferred_element_type=F32)
        m_scr[hh, mi, :, qs] = m_new

    def sweep(near):
        for hh in range(HEADS_PER_STEP):
            for qh in range(Q_SPLIT):
                for mi in range(2):
                    update(hh, mi, qh, near)

    pl.when(d >= 2)(lambda: sweep(None))
    pl.when(d == 1)(lambda: sweep(1))

    @pl.when(d == 0)
    def _():
        sweep(0)
        lam = _lambda_full(lq1_ref, lk1_ref, lq2_ref, lk2_ref)
        for hh in range(HEADS_PER_STEP):
            ot = (acc_scr[hh, 0] / l_scr[hh, 0]
                  - lam * (acc_scr[hh, 1] / l_scr[hh, 1]))
            ot = ot * lax.rsqrt(jnp.mean(ot * ot, axis=0, keepdims=True) + EPS)
            o_ref[:, hh * D_VHEAD:(hh + 1) * D_VHEAD] = (
                jnp.transpose(ot) * (gh_ref[...] * (1.0 - LAM_INIT))).astype(BF16)


def _attn_prompt(q, k, v, dtiles, lams, gh, *, batch, seq):
    nq = seq // T_ATT
    hps = HEADS_PER_STEP
    tri = [(qi, ki) for qi in range(nq) for ki in range(qi + 1)]
    qi_tab = jnp.asarray([a for a, _ in tri], jnp.int32)
    ki_tab = jnp.asarray([b for _, b in tri], jnp.int32)
    vec = pl.BlockSpec((1, D_HEAD), lambda b, h, t, qt, kt: (0, 0))
    grid_spec = pltpu.PrefetchScalarGridSpec(
        num_scalar_prefetch=2,
        grid=(batch, N_HEADS // hps, len(tri)),
        in_specs=[
            pl.BlockSpec((T_ATT, hps * 2 * D_HEAD), lambda b, h, t, qt, kt: (b * nq + qt[t], h)),
            pl.BlockSpec((T_ATT, hps * 2 * D_HEAD), lambda b, h, t, qt, kt: (b * nq + kt[t], h)),
            pl.BlockSpec((T_ATT, hps * D_VHEAD), lambda b, h, t, qt, kt: (b * nq + kt[t], h)),
            pl.BlockSpec((hps, 2, SUB, SUB), lambda b, h, t, qt, kt: (h, 0, 0, 0)),
            vec, vec, vec, vec,
            pl.BlockSpec((1, D_VHEAD), lambda b, h, t, qt, kt: (0, 0)),
        ],
        out_specs=pl.BlockSpec((T_ATT, hps * D_VHEAD), lambda b, h, t, qt, kt: (b * nq + qt[t], h)),
        scratch_shapes=[pltpu.VMEM((hps, 2, 1, T_ATT), F32),
                        pltpu.VMEM((hps, 2, 1, T_ATT), F32),
                        pltpu.VMEM((hps, 2, D_VHEAD, T_ATT), F32),
                        pltpu.VMEM((hps, 2, T_ATT, T_ATT), F32)],
    )
    return pl.pallas_call(
        _attn_prompt_kernel,
        grid_spec=grid_spec,
        out_shape=jax.ShapeDtypeStruct((batch * seq, N_HEADS * D_VHEAD), BF16),
        compiler_params=pltpu.CompilerParams(
            dimension_semantics=("parallel", "parallel", "arbitrary"),
            vmem_limit_bytes=VMEM_LIMIT),
        name="attn_prompt",
    )(qi_tab, ki_tab, q, k, v, dtiles, *lams, gh)


ROWS_S = 128
PAGES_PER_STEP = 8


def _attn_sample_kernel(pt_ref, q_ref, *refs, n_steps, t_new):
    pps = PAGES_PER_STEP
    page_refs = refs[:4 * pps]
    (knl_ref, knh_ref, vnl_ref, vnh_ref, bias_ref, bnew_ref,
     lq1_ref, lk1_ref, lq2_ref, lk2_ref, gh_ref,
     o_ref, qh_scr, m_scr, l_scr, acc_scr) = refs[4 * pps:]
    p = pl.program_id(1)
    rph = 2 * t_new
    nt = (((1,), (1,)), ((), ()))

    @pl.when(p == 0)
    def _():
        q = q_ref[0]
        zero = jnp.zeros((t_new, D_HEAD), F32)
        for h in range(N_HEADS):
            q1 = q[:, h * 2 * D_HEAD:h * 2 * D_HEAD + D_HEAD]
            q2 = q[:, h * 2 * D_HEAD + D_HEAD:(h + 1) * 2 * D_HEAD]
            qh_scr[h] = jnp.concatenate(
                [jnp.concatenate([q1, zero], axis=1), jnp.concatenate([zero, q2], axis=1)],
                axis=0).astype(BF16)
        m_scr[...] = jnp.full(m_scr.shape, -jnp.inf, F32)
        l_scr[...] = jnp.zeros(l_scr.shape, F32)
        acc_scr[...] = jnp.zeros(acc_scr.shape, F32)

    def head_rows(lo_ref, hi_ref, h, n):
        rows = pl.ds(h, n, stride=N_HEADS)
        return jnp.concatenate([lo_ref[rows, :], hi_ref[rows, :]], axis=1)

    def attend(kh, vh, bias):
        s = jnp.concatenate(
            [lax.dot_general(qh_scr[h], kh[h], nt, preferred_element_type=F32)
             for h in range(N_HEADS)], axis=0) + bias
        m_old = m_scr[...]
        m_new = jnp.maximum(m_old, jnp.max(s, axis=-1, keepdims=True))
        alpha = jnp.exp2(m_old - m_new)
        pr = jnp.exp2(s - m_new)
        l_scr[...] = alpha * l_scr[...] + jnp.sum(pr, axis=-1, keepdims=True)
        m_scr[...] = m_new
        prb = pr.astype(BF16)
        for h in range(N_HEADS):
            rs = slice(h * rph, (h + 1) * rph)
            pv = jnp.dot(prb[rs], vh[h], preferred_element_type=F32)
            acc_scr[h] = alpha[rs] * acc_scr[h] + pv

    kh, vh = [], []
    for h in range(N_HEADS):
        kh.append(jnp.concatenate(
            [head_rows(page_refs[4 * r], page_refs[4 * r + 1], h, PAGE) for r in range(pps)],
            axis=0).astype(BF16))
        vh.append(jnp.concatenate(
            [head_rows(page_refs[4 * r + 2], page_refs[4 * r + 3], h, PAGE) for r in range(pps)],
            axis=0).astype(BF16))
    attend(kh, vh, bias_ref[0])

    @pl.when(p == n_steps - 1)
    def _():
        pad = jnp.zeros((PAGE - t_new, D_VHEAD), F32)
        kn = [jnp.concatenate([head_rows(knl_ref, knh_ref, h, t_new), pad], axis=0).astype(BF16)
              for h in range(N_HEADS)]
        vn = [jnp.concatenate([head_rows(vnl_ref, vnh_ref, h, t_new), pad], axis=0).astype(BF16)
              for h in range(N_HEADS)]
        attend(kn, vn, bnew_ref[...])
        lam = _lambda_full(lq1_ref, lk1_ref, lq2_ref, lk2_ref)
        linv = 1.0 / l_scr[...]
        for h in range(N_HEADS):
            a = acc_scr[h] * linv[h * rph:(h + 1) * rph]
            o = a[:t_new] - lam * a[t_new:]
            o_ref[0, :, h * D_VHEAD:(h + 1) * D_VHEAD] = _head_norm(o, gh_ref[...])


def _attn_sample(page_table, q, cache_k, cache_v, k_new, v_new, bias, bias_new, lams, gh):
    n_seq, t_new, width = q.shape
    pps = PAGES_PER_STEP
    n_steps = page_table.shape[1] // pps
    assert n_steps * pps == page_table.shape[1]
    seq_blk = pl.BlockSpec((1, t_new, width), lambda s, p, pt: (s, 0, 0))
    vec = pl.BlockSpec((1, D_HEAD), lambda s, p, pt: (0, 0))

    def half_blk(rows, index, half):
        return pl.BlockSpec((rows * N_HEADS, D_HEAD), lambda s, p, pt: (index(s, p, pt), half))

    page_specs, page_args = [], []
    for r in range(pps):
        idx = lambda s, p, pt, r=r: pt[s, p * pps + r]
        for arr in (cache_k, cache_v):
            for half in range(2):
                page_specs.append(half_blk(PAGE, idx, half))
                page_args.append(arr)
    new_idx = lambda s, p, pt: s
    new_specs = [half_blk(t_new, new_idx, half) for _ in range(2) for half in range(2)]

    grid_spec = pltpu.PrefetchScalarGridSpec(
        num_scalar_prefetch=1,
        grid=(n_seq, n_steps),
        in_specs=[seq_blk] + page_specs + new_specs + [
            pl.BlockSpec((1, ROWS_S, pps * PAGE),
                         lambda s, p, pt: (jnp.where(p == n_steps - 1, 1, 0), 0, 0)),
            pl.BlockSpec((ROWS_S, PAGE), lambda s, p, pt: (0, 0)),
            vec, vec, vec, vec,
            pl.BlockSpec((1, D_VHEAD), lambda s, p, pt: (0, 0)),
        ],
        out_specs=seq_blk,
        scratch_shapes=[pltpu.VMEM((N_HEADS, 2 * t_new, 2 * D_HEAD), BF16),
                        pltpu.VMEM((ROWS_S, 1), F32),
                        pltpu.VMEM((ROWS_S, 1), F32),
                        pltpu.VMEM((N_HEADS, 2 * t_new, D_VHEAD), F32)],
    )
    return pl.pallas_call(
        functools.partial(_attn_sample_kernel, n_steps=n_steps, t_new=t_new),
        grid_spec=grid_spec,
        out_shape=jax.ShapeDtypeStruct((n_seq, t_new, width), F32),
        compiler_params=pltpu.CompilerParams(
            dimension_semantics=("parallel", "arbitrary"),
            vmem_limit_bytes=VMEM_LIMIT),
        name="attn_sample",
    )(page_table, q, *page_args, k_new, k_new, v_new, v_new, bias, bias_new, *lams, gh)


def _mix_kernel(sv_ref, u_ref, oap_ref, oas_ref, ga_ref, gm_ref, w_ref, b_ref, o_ref, *,
                prompt_chunks):
    is_prompt = pl.program_id(0) < prompt_chunks
    for g in range(GROUPS):
        cs = slice(g * GDIM, (g + 1) * GDIM)
        mixed = jnp.dot(w_ref[0, g], sv_ref[:, cs].astype(BF16),
                        preferred_element_type=F32) + b_ref[0, :, cs]
        oa = jnp.where(is_prompt, oap_ref[:, cs].astype(F32), oas_ref[:, cs])
        merged = (ga_ref[:, cs].astype(F32) * oa
                  + gm_ref[:, cs].astype(F32) * (u_ref[:, cs].astype(F32) * mixed))
        o_ref[:, cs] = merged.astype(BF16)


def _mix(sv, u, oa_prompt, oa_sample, gates, w_mix, b_mix):
    rows = sv.shape[0]
    prompt_chunks = oa_prompt.shape[0] // CHUNK
    blk = pl.BlockSpec((CHUNK, D_MODEL), lambda c: (c, 0))
    kind = lambda c: jnp.where(c >= prompt_chunks, 1, 0)
    return pl.pallas_call(
        functools.partial(_mix_kernel, prompt_chunks=prompt_chunks),
        grid=(rows // CHUNK,),
        in_specs=[
            blk, blk,
            pl.BlockSpec((CHUNK, D_MODEL), lambda c: (jnp.minimum(c, prompt_chunks - 1), 0)),
            pl.BlockSpec((CHUNK, D_MODEL), lambda c: (jnp.maximum(c - prompt_chunks, 0), 0)),
            pl.BlockSpec((CHUNK, D_MODEL), lambda c: (c, 0)),
            pl.BlockSpec((CHUNK, D_MODEL), lambda c: (c, 1)),
            pl.BlockSpec((1, GROUPS, CHUNK, CHUNK), lambda c: (kind(c), 0, 0, 0)),
            pl.BlockSpec((1, CHUNK, D_MODEL), lambda c: (kind(c), 0, 0)),
        ],
        out_specs=blk,
        out_shape=jax.ShapeDtypeStruct((rows, D_MODEL), BF16),
        compiler_params=pltpu.CompilerParams(
            dimension_semantics=("parallel",),
            vmem_limit_bytes=VMEM_LIMIT),
        name="gmlp_mix_merge",
    )(sv, u, oa_prompt, oa_sample, gates, gates, w_mix, b_mix)


def _out_proj_kernel(x_ref, w_ref, g_ref, h_ref, o_ref):
    w = w_ref[...]
    sub = TM_OUT // 2
    for r in range(2):
        rs = slice(r * sub, (r + 1) * sub)
        z = jnp.dot(x_ref[rs], w, preferred_element_type=F32)
        o_ref[rs] = h_ref[rs] + _rms_rows(z, g_ref[...])


def _out_proj(merged, w_o, g_post, h):
    rows = merged.shape[0]
    blk = pl.BlockSpec((TM_OUT, D_MODEL), lambda i: (i, 0))
    return pl.pallas_call(
        _out_proj_kernel,
        grid=(rows // TM_OUT,),
        in_specs=[blk,
                  pl.BlockSpec((D_MODEL, D_MODEL), lambda i: (0, 0)),
                  pl.BlockSpec((1, D_MODEL), lambda i: (0, 0)),
                  blk],
        out_specs=blk,
        out_shape=jax.ShapeDtypeStruct((rows, D_MODEL), F32),
        compiler_params=pltpu.CompilerParams(
            dimension_semantics=("parallel",),
            vmem_limit_bytes=VMEM_LIMIT),
        name="out_proj",
    )(merged, w_o, g_post, h)


def kernel(x_prompt, x_sample, cache_k, cache_v, page_table, rel_bias, ffn1_norm_pre, ffn1_w_gate, ffn1_w_up, ffn1_w_down, ffn1_norm_post, mix_norm_pre, w_in, lambda_q1, lambda_k1, lambda_q2, lambda_k2, attn_head_norm, gmlp_ln_g, gmlp_ln_b, gmlp_w_s, gmlp_b_s, w_o, mix_norm_post, ffn2_norm_pre, ffn2_w_gate, ffn2_w_up, ffn2_w_down, ffn2_norm_post):
    batch, seq, _ = x_prompt.shape
    n_seq, t_new, _ = x_sample.shape
    depth = cache_k.shape[0]
    assert depth == 1
    rows_p = batch * seq
    rows_s = n_seq * t_new
    rows = rows_p + rows_s
    assert t_new * 2 * N_HEADS == ROWS_S and CHUNK % t_new == 0

    row = lambda a: a.reshape(1, -1).astype(F32)
    bf = lambda a: a[0].astype(BF16)

    h1, n1 = _ffn(x_prompt.reshape(rows_p, D_MODEL), row(ffn1_norm_pre), bf(ffn1_w_gate),
                  bf(ffn1_w_up), bf(ffn1_w_down), row(ffn1_norm_post), row(mix_norm_pre),
                  row0=0, rows=rows, emit_next=True, x2=x_sample.reshape(rows_s, D_MODEL))

    w_in_b = bf(w_in)
    (q_p,) = _proj(n1, w_in_b, sec0=0, nsec=1, row0=0, rows=rows_p, mode="q_bf16")
    (q_s,) = _proj(n1, w_in_b, sec0=0, nsec=1, row0=rows_p, rows=rows_s, mode="q_f32")
    k_p, k_pb = _proj(n1, w_in_b, sec0=1, nsec=1, row0=0, rows=rows_p, mode="kv")
    k_s, _ = _proj(n1, w_in_b, sec0=1, nsec=1, row0=rows_p, rows=rows_s, mode="kv")
    v_p, v_pb = _proj(n1, w_in_b, sec0=2, nsec=1, row0=0, rows=rows_p, mode="kv")
    v_s, _ = _proj(n1, w_in_b, sec0=2, nsec=1, row0=rows_p, rows=rows_s, mode="kv")
    (u,) = _proj(n1, w_in_b, sec0=3, nsec=1, row0=0, rows=rows, mode="gelu")
    (sv,) = _proj(n1, w_in_b, sec0=4, nsec=1, row0=0, rows=rows, mode="gelu_ln",
                  extra=(row(gmlp_ln_g), row(gmlp_ln_b)))
    (gates,) = _proj(n1, w_in_b, sec0=5, nsec=2, row0=0, rows=rows, mode="sigmoid")

    tbl = _bias_by_distance(rel_bias, SUB + 1)
    g = ((tbl - rel_bias[N_BUCKETS - 1][None, :].astype(F32)) * LOG2E).T
    vec = jnp.concatenate([g[:, ::-1], jnp.zeros((N_HEADS, SUB), F32)], axis=1)
    a = jnp.tile(vec, (1, SUB))[:, :SUB * 2 * SUB].reshape(N_HEADS, SUB, 2 * SUB)
    ii = jnp.arange(SUB)
    d1 = a[:, :, :SUB]
    d0 = jnp.where(ii[:, None] >= ii[None, :], a[:, :, SUB:], -jnp.inf)
    dtiles = jnp.stack([d0, d1], axis=1)

    expand = lambda b: jnp.broadcast_to(b[:, None], (N_HEADS, 2) + b.shape[1:]).reshape(ROWS_S, -1)
    b_last = expand(d1[:, :t_new, :])
    far = jnp.zeros((ROWS_S, (PAGES_PER_STEP - 1) * PAGE), F32)
    bias_s = jnp.stack([jnp.zeros((ROWS_S, PAGES_PER_STEP * PAGE), F32),
                        jnp.concatenate([far, b_last], axis=1)])
    b_new = jnp.concatenate([d0[:, :t_new, :t_new],
                             jnp.full((N_HEADS, t_new, PAGE - t_new), -jnp.inf, F32)], axis=2)
    bias_new = expand(b_new)

    lams = (row(lambda_q1), row(lambda_k1), row(lambda_q2), row(lambda_k2))
    gh = row(attn_head_norm)

    oa_p = _attn_prompt(q_p, k_pb, v_pb, jnp.swapaxes(dtiles, 2, 3), lams, gh, batch=batch, seq=seq)
    width = N_HEADS * D_VHEAD
    oa_s = _attn_sample(page_table,
                        q_s.reshape(n_seq, t_new, width),
                        cache_k.reshape(-1, D_VHEAD),
                        cache_v.reshape(-1, D_VHEAD),
                        k_s.reshape(-1, D_VHEAD), v_s.reshape(-1, D_VHEAD),
                        bias_s, bias_new, lams, gh)

    w_s = gmlp_w_s[0].astype(F32)
    w_prompt = jnp.tril(w_s)
    w_small = jnp.tril(w_s[:, :t_new, :t_new])
    eye = jnp.eye(CHUNK // t_new, dtype=F32)
    w_sample = jnp.einsum("ab,gts->gatbs", eye, w_small).reshape(GROUPS, CHUNK, CHUNK)
    w_mix = jnp.stack([w_prompt, w_sample]).astype(BF16)
    b_s = gmlp_b_s[0].astype(F32)
    b_prompt = jnp.repeat(b_s.T, GDIM, axis=1)
    b_sample = jnp.tile(jnp.repeat(b_s[:, :t_new].T, GDIM, axis=1), (CHUNK // t_new, 1))
    b_mix = jnp.stack([b_prompt, b_sample])

    merged = _mix(sv, u, oa_p, oa_s.reshape(rows_s, width), gates, w_mix, b_mix)
    h2 = _out_proj(merged, bf(w_o), row(mix_norm_post), h1)

    f2 = (row(ffn2_norm_pre), bf(ffn2_w_gate), bf(ffn2_w_up), bf(ffn2_w_down),
          row(ffn2_norm_post), row(ffn2_norm_post))
    (y_p,) = _ffn(h2, *f2, row0=0, rows=rows_p, emit_next=False)
    (y_s,) = _ffn(h2, *f2, row0=rows_p, rows=rows_s, emit_next=False)

    return (y_p.reshape(batch, seq, D_MODEL),
            y_s.reshape(n_seq, t_new, D_MODEL),
            k_p.reshape(1, batch, seq, N_HEADS, 2 * D_HEAD),
            v_p.reshape(1, batch, seq, N_HEADS, D_VHEAD),
            k_s.reshape(1, n_seq, t_new, N_HEADS, 2 * D_HEAD),
            v_s.reshape(1, n_seq, t_new, N_HEADS, D_VHEAD),
            sv[rows_p:].reshape(1, n_seq, t_new, D_MODEL))
```

```python
import functools
import math

import jax
import jax.numpy as jnp
from jax import lax
from jax.experimental import pallas as pl
from jax.experimental.pallas import tpu as pltpu

F32 = jnp.float32
BF16 = jnp.bfloat16

D_MODEL = 2048
D_FF = 5632
N_HEADS = 8
D_HEAD = 128
D_VHEAD = 256
SEC_W = 2048
N_BUCKETS = 32
MAX_DISTANCE = 128
PAGE = 128
CHUNK = 128
GROUPS = 16
GDIM = 128
EPS = 1e-6
SCALE = D_HEAD ** -0.5
LOG2E = math.log2(math.e)
QSCALE = SCALE * LOG2E
LAM_INIT = 0.8 - 0.6 * math.exp(-0.3 * 0)

VMEM_LIMIT = 56 * 1024 * 1024
VMEM_LIMIT_FFN = 60 * 1024 * 1024

TM_FFN = 1024
FFN_SPLIT = 2
TF_FFN = 512
TM_PROJ = 1024
PROJ_SPLIT = 4
T_ATT = 512


def _rms_rows(x, g):
    return x * lax.rsqrt(jnp.mean(x * x, axis=-1, keepdims=True) + EPS) * g


def _ffn_kernel(*refs, emit_next, n_first):
    if n_first is None:
        x_ref, *refs = refs
        x2_ref = None
    else:
        x_ref, x2_ref, *refs = refs
    gpre_ref, wg_ref, wu_ref, wd_ref, gpost_ref, gnext_ref, h_ref, xn_ref = refs[:8]
    n_ref = xn_ref if emit_next else None
    j = pl.program_id(1)

    def with_x(cond, fn):
        if x2_ref is None:
            pl.when(cond)(lambda: fn(x_ref))
        else:
            first = pl.program_id(0) < n_first
            pl.when(cond & first)(lambda: fn(x_ref))
            pl.when(cond & jnp.logical_not(first))(lambda: fn(x2_ref))

    def prologue(xr):
        xn_ref[...] = _rms_rows(xr[...], gpre_ref[...]).astype(BF16)
        h_ref[...] = jnp.zeros(h_ref.shape, F32)

    with_x(j == 0, prologue)

    wg, wu, wd = wg_ref[...], wu_ref[...], wd_ref[...]
    sub = TM_FFN // FFN_SPLIT
    for r in range(FFN_SPLIT):
        rs = slice(r * sub, (r + 1) * sub)
        xn = xn_ref[rs]
        g = jnp.dot(xn, wg, preferred_element_type=F32)
        u = jnp.dot(xn, wu, preferred_element_type=F32)
        a = (g * jax.nn.sigmoid(g) * u).astype(BF16)
        h_ref[rs] += jnp.dot(a, wd, preferred_element_type=F32)

    def epilogue(xr):
        h = xr[...] + 0.5 * _rms_rows(h_ref[...], gpost_ref[...])
        h_ref[...] = h
        if emit_next:
            n_ref[...] = _rms_rows(h, gnext_ref[...]).astype(BF16)

    with_x(j == pl.num_programs(1) - 1, epilogue)


def _ffn(x, gpre, wg, wu, wd, gpost, gnext, *, row0, rows, emit_next, x2=None):
    if x2 is not None:
        assert row0 == 0 and rows == x.shape[0] + x2.shape[0]
    nb = rows // TM_FFN
    b0 = row0 // TM_FFN
    nf = D_FF // TF_FFN
    n_first = None if x2 is None else x.shape[0] // TM_FFN
    vec = pl.BlockSpec((1, D_MODEL), lambda i, j: (0, 0))
    row_blk = pl.BlockSpec((TM_FFN, D_MODEL), lambda i, j: (i, 0))
    if x2 is None:
        x_specs = [pl.BlockSpec((TM_FFN, D_MODEL), lambda i, j: (i + b0, 0),
                                pipeline_mode=pl.Buffered(1))]
        xs = [x]
    else:
        x_specs = [pl.BlockSpec((TM_FFN, D_MODEL), lambda i, j: (jnp.minimum(i, n_first - 1), 0),
                                pipeline_mode=pl.Buffered(1)),
                   pl.BlockSpec((TM_FFN, D_MODEL), lambda i, j: (jnp.maximum(i - n_first, 0), 0),
                                pipeline_mode=pl.Buffered(1))]
        xs = [x, x2]
    out_shape = [jax.ShapeDtypeStruct((rows, D_MODEL), F32)]
    out_specs = [row_blk]
    scratch = []
    if emit_next:
        out_shape.append(jax.ShapeDtypeStruct((rows, D_MODEL), BF16))
        out_specs.append(row_blk)
    else:
        scratch.append(pltpu.VMEM((TM_FFN, D_MODEL), BF16))
    res = pl.pallas_call(
        functools.partial(_ffn_kernel, emit_next=emit_next, n_first=n_first),
        grid=(nb, nf),
        in_specs=x_specs + [
            vec,
            pl.BlockSpec((D_MODEL, TF_FFN), lambda i, j: (0, j)),
            pl.BlockSpec((D_MODEL, TF_FFN), lambda i, j: (0, j)),
            pl.BlockSpec((TF_FFN, D_MODEL), lambda i, j: (j, 0)),
            vec,
            vec,
        ],
        out_specs=out_specs,
        out_shape=out_shape,
        scratch_shapes=scratch,
        compiler_params=pltpu.CompilerParams(
            dimension_semantics=("parallel", "arbitrary"),
            vmem_limit_bytes=VMEM_LIMIT_FFN),
        name="ffn_emit_next" if emit_next else "ffn",
    )(*xs, gpre, wg, wu, wd, gpost, gnext)
    return res


def _gelu(x):
    return 0.5 * x * (1.0 + lax.erf(x * (1.0 / math.sqrt(2.0))))


def _proj_kernel(n_ref, w_ref, *rest, mode):
    w = w_ref[...]
    sub = TM_PROJ // PROJ_SPLIT
    for r in range(PROJ_SPLIT):
        rs = slice(r * sub, (r + 1) * sub)
        z = jnp.dot(n_ref[rs], w, preferred_element_type=F32)
        if mode == "q_bf16":
            (o_ref,) = rest
            o_ref[rs] = (z * QSCALE).astype(BF16)
        elif mode == "q_f32":
            (o_ref,) = rest
            o_ref[rs] = z * QSCALE
        elif mode == "kv":
            o_ref, ob_ref = rest
            for h in range(N_HEADS):
                o_ref[rs, h, :] = z[:, h * D_VHEAD:(h + 1) * D_VHEAD]
            ob_ref[rs] = z.astype(BF16)
        elif mode == "gelu":
            (o_ref,) = rest
            o_ref[rs] = _gelu(z).astype(BF16)
        elif mode == "gelu_ln":
            g_ref, b_ref, o_ref = rest
            a = _gelu(z)
            mu = jnp.mean(a, axis=-1, keepdims=True)
            ac = a - mu
            y = ac * lax.rsqrt(jnp.mean(ac * ac, axis=-1, keepdims=True) + EPS)
            o_ref[rs] = y * g_ref[...] + b_ref[...]
        elif mode == "sigmoid":
            (o_ref,) = rest
            o_ref[rs] = jax.nn.sigmoid(z).astype(BF16)
        else:
            raise ValueError(mode)


def _proj(n, w_in, *, sec0, nsec, row0, rows, mode, extra=()):
    nb = rows // TM_PROJ
    b0 = row0 // TM_PROJ
    blk = pl.BlockSpec((TM_PROJ, SEC_W), lambda s, i: (i, s))
    if mode in ("q_bf16", "gelu", "sigmoid"):
        out_dtypes = [BF16]
    elif mode == "kv":
        out_dtypes = [F32, BF16]
    else:
        out_dtypes = [F32]
    out_shape = [jax.ShapeDtypeStruct((rows, nsec * SEC_W), dt) for dt in out_dtypes]
    out_specs = [blk] * len(out_dtypes)
    if mode == "kv":
        assert nsec == 1
        out_shape[0] = jax.ShapeDtypeStruct((rows, N_HEADS, D_VHEAD), F32)
        out_specs[0] = pl.BlockSpec((TM_PROJ, N_HEADS, D_VHEAD), lambda s, i: (i, 0, 0))
    vec = pl.BlockSpec((1, SEC_W), lambda s, i: (0, 0))
    res = pl.pallas_call(
        functools.partial(_proj_kernel, mode=mode),
        grid=(nsec, nb),
        in_specs=[
            pl.BlockSpec((TM_PROJ, D_MODEL), lambda s, i: (i + b0, 0)),
            pl.BlockSpec((D_MODEL, SEC_W), lambda s, i: (0, s + sec0)),
        ] + [vec] * len(extra),
        out_specs=out_specs,
        out_shape=out_shape,
        compiler_params=pltpu.CompilerParams(
            dimension_semantics=("parallel", "parallel"),
            vmem_limit_bytes=VMEM_LIMIT),
        name="proj_" + mode,
    )(n, w_in, *extra)
    return res


def _lambda_full(lq1_ref, lk1_ref, lq2_ref, lk2_ref):
    s1 = jnp.sum(lq1_ref[...] * lk1_ref[...], axis=-1, keepdims=True)
    s2 = jnp.sum(lq2_ref[...] * lk2_ref[...], axis=-1, keepdims=True)
    return jnp.exp(s1) - jnp.exp(s2) + LAM_INIT


def _head_norm(o, gh):
    return _rms_rows(o, gh) * (1.0 - LAM_INIT)


def _bias_by_distance(rel_bias, nmax):
    n = jnp.arange(nmax)
    max_exact = N_BUCKETS // 2
    nf = jnp.maximum(n, 1).astype(F32)
    large = max_exact + (jnp.log(nf / max_exact) / math.log(MAX_DISTANCE / max_exact)
                         * (N_BUCKETS - max_exact)).astype(jnp.int32)
    large = jnp.minimum(large, N_BUCKETS - 1)
    bucket = jnp.where(n < max_exact, n, large)
    return rel_bias[bucket].astype(F32)


SUB = 128
Q_SPLIT = 2
HEADS_PER_STEP = 4


def _attn_prompt_kernel(qi_tab, ki_tab, q_ref, k_ref, v_ref, dt_ref,
                        lq1_ref, lk1_ref, lq2_ref, lk2_ref, gh_ref,
                        o_ref, m_scr, l_scr, acc_scr, bias_scr):
    t = pl.program_id(2)
    qi = qi_tab[t]
    ki = ki_tab[t]
    d = qi - ki

    @pl.when(t == 0)
    def _():
        neg = jnp.full((SUB, SUB), -jnp.inf, F32)
        zero = jnp.zeros((SUB, SUB), F32)
        nsub = T_ATT // SUB
        for hh in range(HEADS_PER_STEP):
            d0 = dt_ref[hh, 0]
            d1 = dt_ref[hh, 1]
            for a in range(nsub):
                for b in range(nsub):
                    ks, qs = slice(b * SUB, (b + 1) * SUB), slice(a * SUB, (a + 1) * SUB)
                    bias_scr[hh, 0, ks, qs] = (d0 if a == b else d1 if a == b + 1
                                               else neg if a < b else zero)
                    bias_scr[hh, 1, ks, qs] = d1 if (a == 0 and b == nsub - 1) else zero

    @pl.when(ki == 0)
    def _():
        m_scr[...] = jnp.full(m_scr.shape, -jnp.inf, F32)
        l_scr[...] = jnp.zeros(l_scr.shape, F32)
        acc_scr[...] = jnp.zeros(acc_scr.shape, F32)

    nt = (((1,), (1,)), ((), ()))
    tn = (((0,), (0,)), ((), ()))
    qw = T_ATT // Q_SPLIT

    def update(hh, mi, qh, near):
        c0 = hh * 2 * D_HEAD + mi * D_HEAD
        qs = slice(qh * qw, (qh + 1) * qw)
        kr = slice(0, (qh + 1) * qw if near == 0 else T_ATT)
        s = lax.dot_general(k_ref[kr, c0:c0 + D_HEAD], q_ref[qs, c0:c0 + D_HEAD], nt,
                            preferred_element_type=F32)
        if near is not None:
            s = s + bias_scr[hh, near, kr, qs]
        m_old = m_scr[hh, mi, :, qs]
        m_new = jnp.maximum(m_old, jnp.max(s, axis=0, keepdims=True))
        alpha = jnp.exp2(m_old - m_new)
        p = jnp.exp2(s - m_new)
        l_scr[hh, mi, :, qs] = alpha * l_scr[hh, mi, :, qs] + jnp.sum(p, axis=0, keepdims=True)
        acc_scr[hh, mi, :, qs] = alpha * acc_scr[hh, mi, :, qs] + lax.dot_general(
            v_ref[kr, hh * D_VHEAD:(hh + 1) * D_VHEAD], p.astype(BF16), tn,
            preferred_element_type=F32)
        m_scr[hh, mi, :, qs] = m_new

    def sweep(near):
        for hh in range(HEADS_PER_STEP):
            for qh in range(Q_SPLIT):
                for mi in range(2):
                    update(hh, mi, qh, near)

    pl.when(d >= 2)(lambda: sweep(None))
    pl.when(d == 1)(lambda: sweep(1))

    @pl.when(d == 0)
    def _():
        sweep(0)
        lam = _lambda_full(lq1_ref, lk1_ref, lq2_ref, lk2_ref)
        for hh in range(HEADS_PER_STEP):
            ot = (acc_scr[hh, 0] / l_scr[hh, 0]
                  - lam * (acc_scr[hh, 1] / l_scr[hh, 1]))
            ot = ot * lax.rsqrt(jnp.mean(ot * ot, axis=0, keepdims=True) + EPS)
            o_ref[:, hh * D_VHEAD:(hh + 1) * D_VHEAD] = (
                jnp.transpose(ot) * (gh_ref[...] * (1.0 - LAM_INIT))).astype(BF16)


def _attn_prompt(q, k, v, dtiles, lams, gh, *, batch, seq):
    nq = seq // T_ATT
    hps = HEADS_PER_STEP
    tri = [(qi, ki) for qi in range(nq) for ki in range(qi + 1)]
    qi_tab = jnp.asarray([a for a, _ in tri], jnp.int32)
    ki_tab = jnp.asarray([b for _, b in tri], jnp.int32)
    vec = pl.BlockSpec((1, D_HEAD), lambda b, h, t, qt, kt: (0, 0))
    grid_spec = pltpu.PrefetchScalarGridSpec(
        num_scalar_prefetch=2,
        grid=(batch, N_HEADS // hps, len(tri)),
        in_specs=[
            pl.BlockSpec((T_ATT, hps * 2 * D_HEAD), lambda b, h, t, qt, kt: (b * nq + qt[t], h)),
            pl.BlockSpec((T_ATT, hps * 2 * D_HEAD), lambda b, h, t, qt, kt: (b * nq + kt[t], h)),
            pl.BlockSpec((T_ATT, hps * D_VHEAD), lambda b, h, t, qt, kt: (b * nq + kt[t], h)),
            pl.BlockSpec((hps, 2, SUB, SUB), lambda b, h, t, qt, kt: (h, 0, 0, 0)),
            vec, vec, vec, vec,
            pl.BlockSpec((1, D_VHEAD), lambda b, h, t, qt, kt: (0, 0)),
        ],
        out_specs=pl.BlockSpec((T_ATT, hps * D_VHEAD), lambda b, h, t, qt, kt: (b * nq + qt[t], h)),
        scratch_shapes=[pltpu.VMEM((hps, 2, 1, T_ATT), F32),
                        pltpu.VMEM((hps, 2, 1, T_ATT), F32),
                        pltpu.VMEM((hps, 2, D_VHEAD, T_ATT), F32),
                        pltpu.VMEM((hps, 2, T_ATT, T_ATT), F32)],
    )
    return pl.pallas_call(
        _attn_prompt_kernel,
        grid_spec=grid_spec,
        out_shape=jax.ShapeDtypeStruct((batch * seq, N_HEADS * D_VHEAD), BF16),
        compiler_params=pltpu.CompilerParams(
            dimension_semantics=("parallel", "parallel", "arbitrary"),
            vmem_limit_bytes=VMEM_LIMIT),
        name="attn_prompt",
    )(qi_tab, ki_tab, q, k, v, dtiles, *lams, gh)


ROWS_S = 128
PAGES_PER_STEP = 8


def _attn_sample_kernel(pt_ref, q_ref, *refs, n_steps, t_new):
    pps = PAGES_PER_STEP
    page_refs = refs[:4 * pps]
    (knl_ref, knh_ref, vnl_ref, vnh_ref, bias_ref, bnew_ref,
     lq1_ref, lk1_ref, lq2_ref, lk2_ref, gh_ref,
     o_ref, qh_scr, m_scr, l_scr, acc_scr) = refs[4 * pps:]
    p = pl.program_id(1)
    rph = 2 * t_new
    nt = (((1,), (1,)), ((), ()))

    @pl.when(p == 0)
    def _():
        q = q_ref[0]
        zero = jnp.zeros((t_new, D_HEAD), F32)
        for h in range(N_HEADS):
            q1 = q[:, h * 2 * D_HEAD:h * 2 * D_HEAD + D_HEAD]
            q2 = q[:, h * 2 * D_HEAD + D_HEAD:(h + 1) * 2 * D_HEAD]
            qh_scr[h] = jnp.concatenate(
                [jnp.concatenate([q1, zero], axis=1), jnp.concatenate([zero, q2], axis=1)],
                axis=0).astype(BF16)
        m_scr[...] = jnp.full(m_scr.shape, -jnp.inf, F32)
        l_scr[...] = jnp.zeros(l_scr.shape, F32)
        acc_scr[...] = jnp.zeros(acc_scr.shape, F32)

    def head_rows(lo_ref, hi_ref, h, n):
        rows = pl.ds(h, n, stride=N_HEADS)
        return jnp.concatenate([lo_ref[rows, :], hi_ref[rows, :]], axis=1)

    def attend(kh, vh, bias):
        s = jnp.concatenate(
            [lax.dot_general(qh_scr[h], kh[h], nt, preferred_element_type=F32)
             for h in range(N_HEADS)], axis=0) + bias
        m_old = m_scr[...]
        m_new = jnp.maximum(m_old, jnp.max(s, axis=-1, keepdims=True))
        alpha = jnp.exp2(m_old - m_new)
        pr = jnp.exp2(s - m_new)
        l_scr[...] = alpha * l_scr[...] + jnp.sum(pr, axis=-1, keepdims=True)
        m_scr[...] = m_new
        prb = pr.astype(BF16)
        for h in range(N_HEADS):
            rs = slice(h * rph, (h + 1) * rph)
            pv = jnp.dot(prb[rs], vh[h], preferred_element_type=F32)
            acc_scr[h] = alpha[rs] * acc_scr[h] + pv

    kh, vh = [], []
    for h in range(N_HEADS):
        kh.append(jnp.concatenate(
            [head_rows(page_refs[4 * r], page_refs[4 * r + 1], h, PAGE) for r in range(pps)],
            axis=0).astype(BF16))
        vh.append(jnp.concatenate(
            [head_rows(page_refs[4 * r + 2], page_refs[4 * r + 3], h, PAGE) for r in range(pps)],
            axis=0).astype(BF16))
    attend(kh, vh, bias_ref[0])

    @pl.when(p == n_steps - 1)
    def _():
        pad = jnp.zeros((PAGE - t_new, D_VHEAD), F32)
        kn = [jnp.concatenate([head_rows(knl_ref, knh_ref, h, t_new), pad], axis=0).astype(BF16)
              for h in range(N_HEADS)]
        vn = [jnp.concatenate([head_rows(vnl_ref, vnh_ref, h, t_new), pad], axis=0).astype(BF16)
              for h in range(N_HEADS)]
        attend(kn, vn, bnew_ref[...])
        lam = _lambda_full(lq1_ref, lk1_ref, lq2_ref, lk2_ref)
        linv = 1.0 / l_scr[...]
        for h in range(N_HEADS):
            a = acc_scr[h] * linv[h * rph:(h + 1) * rph]
            o = a[:t_new] - lam * a[t_new:]
            o_ref[0, :, h * D_VHEAD:(h + 1) * D_VHEAD] = _head_norm(o, gh_ref[...])


def _attn_sample(page_table, q, cache_k, cache_v, k_new, v_new, bias, bias_new, lams, gh):
    n_seq, t_new, width = q.shape
    pps = PAGES_PER_STEP
    n_steps = page_table.shape[1] // pps
    assert n_steps * pps == page_table.shape[1]
    seq_blk = pl.BlockSpec((1, t_new, width), lambda s, p, pt: (s, 0, 0))
    vec = pl.BlockSpec((1, D_HEAD), lambda s, p, pt: (0, 0))

    def half_blk(rows, index, half):
        return pl.BlockSpec((rows * N_HEADS, D_HEAD), lambda s, p, pt: (index(s, p, pt), half))

    page_specs, page_args = [], []
    for r in range(pps):
        idx = lambda s, p, pt, r=r: pt[s, p * pps + r]
        for arr in (cache_k, cache_v):
            for half in range(2):
                page_specs.append(half_blk(PAGE, idx, half))
                page_args.append(arr)
    new_idx = lambda s, p, pt: s
    new_specs = [half_blk(t_new, new_idx, half) for _ in range(2) for half in range(2)]

    grid_spec = pltpu.PrefetchScalarGridSpec(
        num_scalar_prefetch=1,
        grid=(n_seq, n_steps),
        in_specs=[seq_blk] + page_specs + new_specs + [
            pl.BlockSpec((1, ROWS_S, pps * PAGE),
                         lambda s, p, pt: (jnp.where(p == n_steps - 1, 1, 0), 0, 0)),
            pl.BlockSpec((ROWS_S, PAGE), lambda s, p, pt: (0, 0)),
            vec, vec, vec, vec,
            pl.BlockSpec((1, D_VHEAD), lambda s, p, pt: (0, 0)),
        ],
        out_specs=seq_blk,
        scratch_shapes=[pltpu.VMEM((N_HEADS, 2 * t_new, 2 * D_HEAD), BF16),
                        pltpu.VMEM((ROWS_S, 1), F32),
                        pltpu.VMEM((ROWS_S, 1), F32),
                        pltpu.VMEM((N_HEADS, 2 * t_new, D_VHEAD), F32)],
    )
    return pl.pallas_call(
        functools.partial(_attn_sample_kernel, n_steps=n_steps, t_new=t_new),
        grid_spec=grid_spec,
        out_shape=jax.ShapeDtypeStruct((n_seq, t_new, width), F32),
        compiler_params=pltpu.CompilerParams(
            dimension_semantics=("parallel", "arbitrary"),
            vmem_limit_bytes=VMEM_LIMIT),
        name="attn_sample",
    )(page_table, q, *page_args, k_new, k_new, v_new, v_new, bias, bias_new, *lams, gh)


TM_MIX = 256


def _mix_out_kernel(sv_ref, u_ref, oap_ref, oas_ref, ga_ref, gm_ref, w_ref, b_ref,
                    wo_ref, gpost_ref, h_ref, o_ref, merged_scr, *, prompt_blocks):
    is_prompt = pl.program_id(0) < prompt_blocks
    for c in range(TM_MIX // CHUNK):
        rs = slice(c * CHUNK, (c + 1) * CHUNK)
        for g in range(GROUPS):
            cs = slice(g * GDIM, (g + 1) * GDIM)
            mixed = jnp.dot(w_ref[0, g], sv_ref[rs, cs].astype(BF16),
                            preferred_element_type=F32) + b_ref[0, :, cs]
            oa = jnp.where(is_prompt, oap_ref[rs, cs].astype(F32), oas_ref[rs, cs])
            merged = (ga_ref[rs, cs].astype(F32) * oa
                      + gm_ref[rs, cs].astype(F32) * (u_ref[rs, cs].astype(F32) * mixed))
            merged_scr[rs, cs] = merged.astype(BF16)
    z = jnp.dot(merged_scr[...], wo_ref[...], preferred_element_type=F32)
    o_ref[...] = h_ref[...] + _rms_rows(z, gpost_ref[...])


def _mix_out(sv, u, oa_prompt, oa_sample, gates, w_mix, b_mix, w_o, g_post, h):
    rows = sv.shape[0]
    prompt_blocks = oa_prompt.shape[0] // TM_MIX
    blk = pl.BlockSpec((TM_MIX, D_MODEL), lambda i: (i, 0))
    kind = lambda i: jnp.where(i >= prompt_blocks, 1, 0)
    return pl.pallas_call(
        functools.partial(_mix_out_kernel, prompt_blocks=prompt_blocks),
        grid=(rows // TM_MIX,),
        in_specs=[
            blk, blk,
            pl.BlockSpec((TM_MIX, D_MODEL), lambda i: (jnp.minimum(i, prompt_blocks - 1), 0)),
            pl.BlockSpec((TM_MIX, D_MODEL), lambda i: (jnp.maximum(i - prompt_blocks, 0), 0)),
            pl.BlockSpec((TM_MIX, D_MODEL), lambda i: (i, 0)),
            pl.BlockSpec((TM_MIX, D_MODEL), lambda i: (i, 1)),
            pl.BlockSpec((1, GROUPS, CHUNK, CHUNK), lambda i: (kind(i), 0, 0, 0)),
            pl.BlockSpec((1, CHUNK, D_MODEL), lambda i: (kind(i), 0, 0)),
            pl.BlockSpec((D_MODEL, D_MODEL), lambda i: (0, 0), pipeline_mode=pl.Buffered(1)),
            pl.BlockSpec((1, D_MODEL), lambda i: (0, 0)),
            blk,
        ],
        out_specs=blk,
        out_shape=jax.ShapeDtypeStruct((rows, D_MODEL), F32),
        scratch_shapes=[pltpu.VMEM((TM_MIX, D_MODEL), BF16)],
        compiler_params=pltpu.CompilerParams(
            dimension_semantics=("parallel",),
            vmem_limit_bytes=VMEM_LIMIT),
        name="mix_out_proj",
    )(sv, u, oa_prompt, oa_sample, gates, gates, w_mix, b_mix, w_o, g_post, h)


def kernel(x_prompt, x_sample, cache_k, cache_v, page_table, rel_bias, ffn1_norm_pre, ffn1_w_gate, ffn1_w_up, ffn1_w_down, ffn1_norm_post, mix_norm_pre, w_in, lambda_q1, lambda_k1, lambda_q2, lambda_k2, attn_head_norm, gmlp_ln_g, gmlp_ln_b, gmlp_w_s, gmlp_b_s, w_o, mix_norm_post, ffn2_norm_pre, ffn2_w_gate, ffn2_w_up, ffn2_w_down, ffn2_norm_post):
    batch, seq, _ = x_prompt.shape
    n_seq, t_new, _ = x_sample.shape
    depth = cache_k.shape[0]
    assert depth == 1
    rows_p = batch * seq
    rows_s = n_seq * t_new
    rows = rows_p + rows_s
    assert t_new * 2 * N_HEADS == ROWS_S and CHUNK % t_new == 0

    row = lambda a: a.reshape(1, -1).astype(F32)
    bf = lambda a: a[0].astype(BF16)

    h1, n1 = _ffn(x_prompt.reshape(rows_p, D_MODEL), row(ffn1_norm_pre), bf(ffn1_w_gate),
                  bf(ffn1_w_up), bf(ffn1_w_down), row(ffn1_norm_post), row(mix_norm_pre),
                  row0=0, rows=rows, emit_next=True, x2=x_sample.reshape(rows_s, D_MODEL))

    w_in_b = bf(w_in)
    (q_p,) = _proj(n1, w_in_b, sec0=0, nsec=1, row0=0, rows=rows_p, mode="q_bf16")
    (q_s,) = _proj(n1, w_in_b, sec0=0, nsec=1, row0=rows_p, rows=rows_s, mode="q_f32")
    k_p, k_pb = _proj(n1, w_in_b, sec0=1, nsec=1, row0=0, rows=rows_p, mode="kv")
    k_s, _ = _proj(n1, w_in_b, sec0=1, nsec=1, row0=rows_p, rows=rows_s, mode="kv")
    v_p, v_pb = _proj(n1, w_in_b, sec0=2, nsec=1, row0=0, rows=rows_p, mode="kv")
    v_s, _ = _proj(n1, w_in_b, sec0=2, nsec=1, row0=rows_p, rows=rows_s, mode="kv")
    (u,) = _proj(n1, w_in_b, sec0=3, nsec=1, row0=0, rows=rows, mode="gelu")
    (sv,) = _proj(n1, w_in_b, sec0=4, nsec=1, row0=0, rows=rows, mode="gelu_ln",
                  extra=(row(gmlp_ln_g), row(gmlp_ln_b)))
    (gates,) = _proj(n1, w_in_b, sec0=5, nsec=2, row0=0, rows=rows, mode="sigmoid")

    tbl = _bias_by_distance(rel_bias, SUB + 1)
    g = ((tbl - rel_bias[N_BUCKETS - 1][None, :].astype(F32)) * LOG2E).T
    vec = jnp.concatenate([g[:, ::-1], jnp.zeros((N_HEADS, SUB), F32)], axis=1)
    a = jnp.tile(vec, (1, SUB))[:, :SUB * 2 * SUB].reshape(N_HEADS, SUB, 2 * SUB)
    ii = jnp.arange(SUB)
    d1 = a[:, :, :SUB]
    d0 = jnp.where(ii[:, None] >= ii[None, :], a[:, :, SUB:], -jnp.inf)
    dtiles = jnp.stack([d0, d1], axis=1)

    expand = lambda b: jnp.broadcast_to(b[:, None], (N_HEADS, 2) + b.shape[1:]).reshape(ROWS_S, -1)
    b_last = expand(d1[:, :t_new, :])
    far = jnp.zeros((ROWS_S, (PAGES_PER_STEP - 1) * PAGE), F32)
    bias_s = jnp.stack([jnp.zeros((ROWS_S, PAGES_PER_STEP * PAGE), F32),
                        jnp.concatenate([far, b_last], axis=1)])
    b_new = jnp.concatenate([d0[:, :t_new, :t_new],
                             jnp.full((N_HEADS, t_new, PAGE - t_new), -jnp.inf, F32)], axis=2)
    bias_new = expand(b_new)

    lams = (row(lambda_q1), row(lambda_k1), row(lambda_q2), row(lambda_k2))
    gh = row(attn_head_norm)

    oa_p = _attn_prompt(q_p, k_pb, v_pb, jnp.swapaxes(dtiles, 2, 3), lams, gh, batch=batch, seq=seq)
    width = N_HEADS * D_VHEAD
    oa_s = _attn_sample(page_table,
                        q_s.reshape(n_seq, t_new, width),
                        cache_k.reshape(-1, D_VHEAD),
                        cache_v.reshape(-1, D_VHEAD),
                        k_s.reshape(-1, D_VHEAD), v_s.reshape(-1, D_VHEAD),
                        bias_s, bias_new, lams, gh)

    w_s = gmlp_w_s[0].astype(F32)
    w_prompt = jnp.tril(w_s)
    w_small = jnp.tril(w_s[:, :t_new, :t_new])
    eye = jnp.eye(CHUNK // t_new, dtype=F32)
    w_sample = jnp.einsum("ab,gts->gatbs", eye, w_small).reshape(GROUPS, CHUNK, CHUNK)
    w_mix = jnp.stack([w_prompt, w_sample]).astype(BF16)
    b_s = gmlp_b_s[0].astype(F32)
    b_prompt = jnp.repeat(b_s.T, GDIM, axis=1)
    b_sample = jnp.tile(jnp.repeat(b_s[:, :t_new].T, GDIM, axis=1), (CHUNK // t_new, 1))
    b_mix = jnp.stack([b_prompt, b_sample])

    h2 = _mix_out(sv, u, oa_p, oa_s.reshape(rows_s, width), gates, w_mix, b_mix,
                  bf(w_o), row(mix_norm_post), h1)

    f2 = (row(ffn2_norm_pre), bf(ffn2_w_gate), bf(ffn2_w_up), bf(ffn2_w_down),
          row(ffn2_norm_post), row(ffn2_norm_post))
    (y_p,) = _ffn(h2, *f2, row0=0, rows=rows_p, emit_next=False)
    (y_s,) = _ffn(h2, *f2, row0=rows_p, rows=rows_s, emit_next=False)

    return (y_p.reshape(batch, seq, D_MODEL),
            y_s.reshape(n_seq, t_new, D_MODEL),
            k_p.reshape(1, batch, seq, N_HEADS, 2 * D_HEAD),
            v_p.reshape(1, batch, seq, N_HEADS, D_VHEAD),
            k_s.reshape(1, n_seq, t_new, N_HEADS, 2 * D_HEAD),
            v_s.reshape(1, n_seq, t_new, N_HEADS, D_VHEAD),
            sv[rows_p:].reshape(1, n_seq, t_new, D_MODEL))
```

```python
import functools
import math

import jax
import jax.numpy as jnp
from jax import lax
from jax.experimental import pallas as pl
from jax.experimental.pallas import tpu as pltpu

F32 = jnp.float32
BF16 = jnp.bfloat16

D_MODEL = 2048
D_FF = 5632
N_HEADS = 8
D_HEAD = 128
D_VHEAD = 256
SEC_W = 2048
N_BUCKETS = 32
MAX_DISTANCE = 128
PAGE = 128
CHUNK = 128
GROUPS = 16
GDIM = 128
EPS = 1e-6
SCALE = D_HEAD ** -0.5
LOG2E = math.log2(math.e)
QSCALE = SCALE * LOG2E
LAM_INIT = 0.8 - 0.6 * math.exp(-0.3 * 0)

VMEM_LIMIT = 56 * 1024 * 1024
VMEM_LIMIT_FFN = 60 * 1024 * 1024

TM_FFN = 1024
FFN_SPLIT = 2
TF_FFN = 512
TM_PROJ = 1024
PROJ_SPLIT = 4
T_ATT = 512


def _rms_rows(x, g):
    return x * lax.rsqrt(jnp.mean(x * x, axis=-1, keepdims=True) + EPS) * g


def _ffn_kernel(*refs, emit_next, n_first):
    if n_first is None:
        x_ref, *refs = refs
        x2_ref = None
    else:
        x_ref, x2_ref, *refs = refs
    gpre_ref, wg_ref, wu_ref, wd_ref, gpost_ref, gnext_ref, h_ref, xn_ref = refs[:8]
    n_ref = xn_ref if emit_next else None
    j = pl.program_id(1)

    def with_x(cond, fn):
        if x2_ref is None:
            pl.when(cond)(lambda: fn(x_ref))
        else:
            first = pl.program_id(0) < n_first
            pl.when(cond & first)(lambda: fn(x_ref))
            pl.when(cond & jnp.logical_not(first))(lambda: fn(x2_ref))

    def prologue(xr):
        xn_ref[...] = _rms_rows(xr[...], gpre_ref[...]).astype(BF16)
        h_ref[...] = jnp.zeros(h_ref.shape, F32)

    with_x(j == 0, prologue)

    wg, wu, wd = wg_ref[...], wu_ref[...], wd_ref[...]
    sub = TM_FFN // FFN_SPLIT
    for r in range(FFN_SPLIT):
        rs = slice(r * sub, (r + 1) * sub)
        xn = xn_ref[rs]
        g = jnp.dot(xn, wg, preferred_element_type=F32)
        u = jnp.dot(xn, wu, preferred_element_type=F32)
        a = (g * jax.nn.sigmoid(g) * u).astype(BF16)
        h_ref[rs] += jnp.dot(a, wd, preferred_element_type=F32)

    def epilogue(xr):
        h = xr[...] + 0.5 * _rms_rows(h_ref[...], gpost_ref[...])
        h_ref[...] = h
        if emit_next:
            n_ref[...] = _rms_rows(h, gnext_ref[...]).astype(BF16)

    with_x(j == pl.num_programs(1) - 1, epilogue)


def _ffn(x, gpre, wg, wu, wd, gpost, gnext, *, row0, rows, emit_next, x2=None):
    if x2 is not None:
        assert row0 == 0 and rows == x.shape[0] + x2.shape[0]
    nb = rows // TM_FFN
    b0 = row0 // TM_FFN
    nf = D_FF // TF_FFN
    n_first = None if x2 is None else x.shape[0] // TM_FFN
    vec = pl.BlockSpec((1, D_MODEL), lambda i, j: (0, 0))
    row_blk = pl.BlockSpec((TM_FFN, D_MODEL), lambda i, j: (i, 0))
    if x2 is None:
        x_specs = [pl.BlockSpec((TM_FFN, D_MODEL), lambda i, j: (i + b0, 0),
                                pipeline_mode=pl.Buffered(1))]
        xs = [x]
    else:
        x_specs = [pl.BlockSpec((TM_FFN, D_MODEL), lambda i, j: (jnp.minimum(i, n_first - 1), 0),
                                pipeline_mode=pl.Buffered(1)),
                   pl.BlockSpec((TM_FFN, D_MODEL), lambda i, j: (jnp.maximum(i - n_first, 0), 0),
                                pipeline_mode=pl.Buffered(1))]
        xs = [x, x2]
    out_shape = [jax.ShapeDtypeStruct((rows, D_MODEL), F32)]
    out_specs = [row_blk]
    scratch = []
    if emit_next:
        out_shape.append(jax.ShapeDtypeStruct((rows, D_MODEL), BF16))
        out_specs.append(row_blk)
    else:
        scratch.append(pltpu.VMEM((TM_FFN, D_MODEL), BF16))
    res = pl.pallas_call(
        functools.partial(_ffn_kernel, emit_next=emit_next, n_first=n_first),
        grid=(nb, nf),
        in_specs=x_specs + [
            vec,
            pl.BlockSpec((D_MODEL, TF_FFN), lambda i, j: (0, j)),
            pl.BlockSpec((D_MODEL, TF_FFN), lambda i, j: (0, j)),
            pl.BlockSpec((TF_FFN, D_MODEL), lambda i, j: (j, 0)),
            vec,
            vec,
        ],
        out_specs=out_specs,
        out_shape=out_shape,
        scratch_shapes=scratch,
        compiler_params=pltpu.CompilerParams(
            dimension_semantics=("parallel", "arbitrary"),
            vmem_limit_bytes=VMEM_LIMIT_FFN),
        name="ffn_emit_next" if emit_next else "ffn",
    )(*xs, gpre, wg, wu, wd, gpost, gnext)
    return res


def _gelu(x):
    return 0.5 * x * (1.0 + lax.erf(x * (1.0 / math.sqrt(2.0))))


def _proj_kernel(n_ref, w_ref, *rest, mode):
    w = w_ref[...]
    sub = TM_PROJ // PROJ_SPLIT
    for r in range(PROJ_SPLIT):
        rs = slice(r * sub, (r + 1) * sub)
        z = jnp.dot(n_ref[rs], w, preferred_element_type=F32)
        if mode == "q_bf16":
            (o_ref,) = rest
            o_ref[rs] = (z * QSCALE).astype(BF16)
        elif mode == "q_f32":
            (o_ref,) = rest
            o_ref[rs] = z * QSCALE
        elif mode == "kv":
            o_ref, ob_ref = rest
            for h in range(N_HEADS):
                o_ref[rs, h, :] = z[:, h * D_VHEAD:(h + 1) * D_VHEAD]
            ob_ref[rs] = z.astype(BF16)
        elif mode == "gelu":
            (o_ref,) = rest
            o_ref[rs] = _gelu(z).astype(BF16)
        elif mode == "gelu_ln":
            g_ref, b_ref, o_ref = rest
            a = _gelu(z)
            mu = jnp.mean(a, axis=-1, keepdims=True)
            ac = a - mu
            y = ac * lax.rsqrt(jnp.mean(ac * ac, axis=-1, keepdims=True) + EPS)
            o_ref[rs] = y * g_ref[...] + b_ref[...]
        elif mode == "sigmoid":
            (o_ref,) = rest
            o_ref[rs] = jax.nn.sigmoid(z).astype(BF16)
        else:
            raise ValueError(mode)


def _proj(n, w_in, *, sec0, nsec, row0, rows, mode, extra=()):
    nb = rows // TM_PROJ
    b0 = row0 // TM_PROJ
    blk = pl.BlockSpec((TM_PROJ, SEC_W), lambda s, i: (i, s))
    if mode in ("q_bf16", "gelu", "sigmoid"):
        out_dtypes = [BF16]
    elif mode == "kv":
        out_dtypes = [F32, BF16]
    else:
        out_dtypes = [F32]
    out_shape = [jax.ShapeDtypeStruct((rows, nsec * SEC_W), dt) for dt in out_dtypes]
    out_specs = [blk] * len(out_dtypes)
    if mode == "kv":
        assert nsec == 1
        out_shape[0] = jax.ShapeDtypeStruct((rows, N_HEADS, D_VHEAD), F32)
        out_specs[0] = pl.BlockSpec((TM_PROJ, N_HEADS, D_VHEAD), lambda s, i: (i, 0, 0))
    vec = pl.BlockSpec((1, SEC_W), lambda s, i: (0, 0))
    res = pl.pallas_call(
        functools.partial(_proj_kernel, mode=mode),
        grid=(nsec, nb),
        in_specs=[
            pl.BlockSpec((TM_PROJ, D_MODEL), lambda s, i: (i + b0, 0)),
            pl.BlockSpec((D_MODEL, SEC_W), lambda s, i: (0, s + sec0)),
        ] + [vec] * len(extra),
        out_specs=out_specs,
        out_shape=out_shape,
        compiler_params=pltpu.CompilerParams(
            dimension_semantics=("parallel", "parallel"),
            vmem_limit_bytes=VMEM_LIMIT),
        name="proj_" + mode,
    )(n, w_in, *extra)
    return res


def _lambda_full(lq1_ref, lk1_ref, lq2_ref, lk2_ref):
    s1 = jnp.sum(lq1_ref[...] * lk1_ref[...], axis=-1, keepdims=True)
    s2 = jnp.sum(lq2_ref[...] * lk2_ref[...], axis=-1, keepdims=True)
    return jnp.exp(s1) - jnp.exp(s2) + LAM_INIT


def _head_norm(o, gh):
    return _rms_rows(o, gh) * (1.0 - LAM_INIT)


def _bias_by_distance(rel_bias, nmax):
    n = jnp.arange(nmax)
    max_exact = N_BUCKETS // 2
    nf = jnp.maximum(n, 1).astype(F32)
    large = max_exact + (jnp.log(nf / max_exact) / math.log(MAX_DISTANCE / max_exact)
                         * (N_BUCKETS - max_exact)).astype(jnp.int32)
    large = jnp.minimum(large, N_BUCKETS - 1)
    bucket = jnp.where(n < max_exact, n, large)
    return rel_bias[bucket].astype(F32)


SUB = 128
Q_SPLIT = 2
HEADS_PER_STEP = 4


def _attn_prompt_kernel(qi_tab, ki_tab, q_ref, k_ref, v_ref, dt_ref,
                        lq1_ref, lk1_ref, lq2_ref, lk2_ref, gh_ref,
                        o_ref, m_scr, l_scr, acc_scr, bias_scr):
    t = pl.program_id(2)
    qi = qi_tab[t]
    ki = ki_tab[t]
    d = qi - ki

    @pl.when(t == 0)
    def _():
        neg = jnp.full((SUB, SUB), -jnp.inf, F32)
        zero = jnp.zeros((SUB, SUB), F32)
        nsub = T_ATT // SUB
        for hh in range(HEADS_PER_STEP):
            d0 = dt_ref[hh, 0]
            d1 = dt_ref[hh, 1]
            for a in range(nsub):
                for b in range(nsub):
                    ks, qs = slice(b * SUB, (b + 1) * SUB), slice(a * SUB, (a + 1) * SUB)
                    bias_scr[hh, 0, ks, qs] = (d0 if a == b else d1 if a == b + 1
                                               else neg if a < b else zero)
                    bias_scr[hh, 1, ks, qs] = d1 if (a == 0 and b == nsub - 1) else zero

    @pl.when(ki == 0)
    def _():
        m_scr[...] = jnp.full(m_scr.shape, -jnp.inf, F32)
        l_scr[...] = jnp.zeros(l_scr.shape, F32)
        acc_scr[...] = jnp.zeros(acc_scr.shape, F32)

    nt = (((1,), (1,)), ((), ()))
    tn = (((0,), (0,)), ((), ()))
    qw = T_ATT // Q_SPLIT

    def update(hh, mi, qh, near):
        c0 = hh * 2 * D_HEAD + mi * D_HEAD
        qs = slice(qh * qw, (qh + 1) * qw)
        kr = slice(0, (qh + 1) * qw if near == 0 else T_ATT)
        s = lax.dot_general(k_ref[kr, c0:c0 + D_HEAD], q_ref[qs, c0:c0 + D_HEAD], nt,
                            preferred_element_type=F32)
        if near is not None:
            s = s + bias_scr[hh, near, kr, qs]
        m_old = m_scr[hh, mi, :, qs]
        m_new = jnp.maximum(m_old, jnp.max(s, axis=0, keepdims=True))
        alpha = jnp.exp2(m_old - m_new)
        p = jnp.exp2(s - m_new)
        l_scr[hh, mi, :, qs] = alpha * l_scr[hh, mi, :, qs] + jnp.sum(p, axis=0, keepdims=True)
        acc_scr[hh, mi, :, qs] = alpha * acc_scr[hh, mi, :, qs] + lax.dot_general(
            v_ref[kr, hh * D_VHEAD:(hh + 1) * D_VHEAD], p.astype(BF16), tn,
            preferred_element_type=F32)
        m_scr[hh, mi, :, qs] = m_new

    def sweep(near):
        for hh in range(HEADS_PER_STEP):
            for qh in range(Q_SPLIT):
                for mi in range(2):
                    update(hh, mi, qh, near)

    pl.when(d >= 2)(lambda: sweep(None))
    pl.when(d == 1)(lambda: sweep(1))

    @pl.when(d == 0)
    def _():
        sweep(0)
        lam = _lambda_full(lq1_ref, lk1_ref, lq2_ref, lk2_ref)
        for hh in range(HEADS_PER_STEP):
            ot = (acc_scr[hh, 0] / l_scr[hh, 0]
                  - lam * (acc_scr[hh, 1] / l_scr[hh, 1]))
            ot = ot * lax.rsqrt(jnp.mean(ot * ot, axis=0, keepdims=True) + EPS)
            o_ref[:, hh * D_VHEAD:(hh + 1) * D_VHEAD] = (
                jnp.transpose(ot) * (gh_ref[...] * (1.0 - LAM_INIT))).astype(BF16)


def _attn_prompt(q, k, v, dtiles, lams, gh, *, batch, seq):
    nq = seq // T_ATT
    hps = HEADS_PER_STEP
    tri = [(qi, ki) for qi in range(nq) for ki in range(qi + 1)]
    qi_tab = jnp.asarray([a for a, _ in tri], jnp.int32)
    ki_tab = jnp.asarray([b for _, b in tri], jnp.int32)
    vec = pl.BlockSpec((1, D_HEAD), lambda b, h, t, qt, kt: (0, 0))
    grid_spec = pltpu.PrefetchScalarGridSpec(
        num_scalar_prefetch=2,
        grid=(batch, N_HEADS // hps, len(tri)),
        in_specs=[
            pl.BlockSpec((T_ATT, hps * 2 * D_HEAD), lambda b, h, t, qt, kt: (b * nq + qt[t], h)),
            pl.BlockSpec((T_ATT, hps * 2 * D_HEAD), lambda b, h, t, qt, kt: (b * nq + kt[t], h)),
            pl.BlockSpec((T_ATT, hps * D_VHEAD), lambda b, h, t, qt, kt: (b * nq + kt[t], h)),
            pl.BlockSpec((hps, 2, SUB, SUB), lambda b, h, t, qt, kt: (h, 0, 0, 0)),
            vec, vec, vec, vec,
            pl.BlockSpec((1, D_VHEAD), lambda b, h, t, qt, kt: (0, 0)),
        ],
        out_specs=pl.BlockSpec((T_ATT, hps * D_VHEAD), lambda b, h, t, qt, kt: (b * nq + qt[t], h)),
        scratch_shapes=[pltpu.VMEM((hps, 2, 1, T_ATT), F32),
                        pltpu.VMEM((hps, 2, 1, T_ATT), F32),
                        pltpu.VMEM((hps, 2, D_VHEAD, T_ATT), F32),
                        pltpu.VMEM((hps, 2, T_ATT, T_ATT), F32)],
    )
    return pl.pallas_call(
        _attn_prompt_kernel,
        grid_spec=grid_spec,
        out_shape=jax.ShapeDtypeStruct((batch * seq, N_HEADS * D_VHEAD), BF16),
        compiler_params=pltpu.CompilerParams(
            dimension_semantics=("parallel", "parallel", "arbitrary"),
            vmem_limit_bytes=VMEM_LIMIT),
        name="attn_prompt",
    )(qi_tab, ki_tab, q, k, v, dtiles, *lams, gh)


ROWS_S = 128
PAGES_PER_STEP = 8


def _attn_sample_kernel(pt_ref, q_ref, *refs, n_steps, t_new):
    pps = PAGES_PER_STEP
    page_refs = refs[:2 * pps]
    (kn_ref, vn_ref, blast_ref, bnew_ref,
     lq1_ref, lk1_ref, lq2_ref, lk2_ref, gh_ref,
     o_ref, qt_scr, m_scr, l_scr, acc_scr) = refs[2 * pps:]
    p = pl.program_id(1)
    rph = 2 * t_new
    tn = (((0,), (0,)), ((), ()))

    def to_rows(x):
        return jnp.transpose(jnp.broadcast_to(x, (8, ROWS_S)))[:, :1]

    @pl.when(p == 0)
    def _():
        q = q_ref[0]
        zero = jnp.zeros((t_new, D_HEAD), F32)
        blocks = []
        for h in range(N_HEADS):
            q1 = q[:, h * 2 * D_HEAD:h * 2 * D_HEAD + D_HEAD]
            q2 = q[:, h * 2 * D_HEAD + D_HEAD:(h + 1) * 2 * D_HEAD]
            blocks += [jnp.concatenate([q1, zero], axis=1), jnp.concatenate([zero, q2], axis=1)]
        qt_scr[...] = jnp.transpose(jnp.concatenate(blocks, axis=0)).astype(BF16)
        m_scr[...] = jnp.full(m_scr.shape, -jnp.inf, F32)
        l_scr[...] = jnp.zeros(l_scr.shape, F32)
        acc_scr[...] = jnp.zeros(acc_scr.shape, F32)

    sub = lax.broadcasted_iota(jnp.int32, (N_HEADS, ROWS_S), 0)
    lane = lax.broadcasted_iota(jnp.int32, (N_HEADS, ROWS_S), 1)
    head_mask = jnp.where(sub == lane // rph, 0.0, -jnp.inf).astype(F32)

    def scores(k):
        st = jnp.dot(k, qt_scr[...], preferred_element_type=F32)
        n = st.shape[0]
        return (st.reshape(n // N_HEADS, N_HEADS, ROWS_S) + head_mask[None]).reshape(n, ROWS_S)

    def attend(sts, vs):
        m_old = m_scr[...]
        m_new = m_old
        for st in sts:
            m_new = jnp.maximum(m_new, jnp.max(st, axis=0, keepdims=True))
        alpha = jnp.exp2(m_old - m_new)
        l_new = alpha * l_scr[...]
        pv = None
        for st, v in zip(sts, vs):
            pr = jnp.exp2(st - m_new)
            l_new = l_new + jnp.sum(pr, axis=0, keepdims=True)
            part = lax.dot_general(pr.astype(BF16), v, tn, preferred_element_type=F32)
            pv = part if pv is None else pv + part
        m_scr[...] = m_new
        l_scr[...] = l_new
        acc_scr[...] = to_rows(alpha) * acc_scr[...] + pv

    sts = [scores(page_refs[2 * r][...].astype(BF16)) for r in range(pps)]
    sts[-1] = sts[-1] + jnp.where(p == n_steps - 1, blast_ref[...], 0.0)
    attend(sts, [page_refs[2 * r + 1][...].astype(BF16) for r in range(pps)])

    @pl.when(p == n_steps - 1)
    def _():
        attend([scores(kn_ref[...].astype(BF16)) + bnew_ref[...]], [vn_ref[...].astype(BF16)])
        lam = _lambda_full(lq1_ref, lk1_ref, lq2_ref, lk2_ref)
        a_all = acc_scr[...] / to_rows(l_scr[...])
        for h in range(N_HEADS):
            a = a_all[h * rph:(h + 1) * rph]
            o = a[:t_new] - lam * a[t_new:]
            o_ref[0, :, h * D_VHEAD:(h + 1) * D_VHEAD] = _head_norm(o, gh_ref[...])


def _attn_sample(page_table, q, cache_k, cache_v, k_new, v_new, bias_last, bias_new, lams, gh):
    n_seq, t_new, width = q.shape
    pps = PAGES_PER_STEP
    n_steps = page_table.shape[1] // pps
    assert n_steps * pps == page_table.shape[1]
    seq_blk = pl.BlockSpec((1, t_new, width), lambda s, p, pt: (s, 0, 0))
    vec = pl.BlockSpec((1, D_HEAD), lambda s, p, pt: (0, 0))
    const = lambda shape: pl.BlockSpec(shape, lambda s, p, pt: (0, 0))

    page_specs, page_args = [], []
    for r in range(pps):
        for arr in (cache_k, cache_v):
            page_specs.append(pl.BlockSpec((PAGE * N_HEADS, D_VHEAD),
                                           lambda s, p, pt, r=r: (pt[s, p * pps + r], 0)))
            page_args.append(arr)
    new_blk = pl.BlockSpec((t_new * N_HEADS, D_VHEAD), lambda s, p, pt: (s, 0))

    grid_spec = pltpu.PrefetchScalarGridSpec(
        num_scalar_prefetch=1,
        grid=(n_seq, n_steps),
        in_specs=[seq_blk] + page_specs + [
            new_blk, new_blk,
            const((PAGE * N_HEADS, ROWS_S)), const((t_new * N_HEADS, ROWS_S)),
            vec, vec, vec, vec,
            pl.BlockSpec((1, D_VHEAD), lambda s, p, pt: (0, 0)),
        ],
        out_specs=seq_blk,
        scratch_shapes=[pltpu.VMEM((2 * D_HEAD, ROWS_S), BF16),
                        pltpu.VMEM((1, ROWS_S), F32),
                        pltpu.VMEM((1, ROWS_S), F32),
                        pltpu.VMEM((ROWS_S, D_VHEAD), F32)],
    )
    return pl.pallas_call(
        functools.partial(_attn_sample_kernel, n_steps=n_steps, t_new=t_new),
        grid_spec=grid_spec,
        out_shape=jax.ShapeDtypeStruct((n_seq, t_new, width), F32),
        compiler_params=pltpu.CompilerParams(
            dimension_semantics=("parallel", "arbitrary"),
            vmem_limit_bytes=VMEM_LIMIT),
        name="attn_sample",
    )(page_table, q, *page_args, k_new, v_new, bias_last, bias_new, *lams, gh)


TM_MIX = 256


def _mix_out_kernel(sv_ref, u_ref, oap_ref, oas_ref, ga_ref, gm_ref, w_ref, b_ref,
                    wo_ref, gpost_ref, h_ref, o_ref, merged_scr, *, prompt_blocks):
    is_prompt = pl.program_id(0) < prompt_blocks
    for c in range(TM_MIX // CHUNK):
        rs = slice(c * CHUNK, (c + 1) * CHUNK)
        for g in range(GROUPS):
            cs = slice(g * GDIM, (g + 1) * GDIM)
            mixed = jnp.dot(w_ref[0, g], sv_ref[rs, cs].astype(BF16),
                            preferred_element_type=F32) + b_ref[0, :, cs]
            oa = jnp.where(is_prompt, oap_ref[rs, cs].astype(F32), oas_ref[rs, cs])
            merged = (ga_ref[rs, cs].astype(F32) * oa
                      + gm_ref[rs, cs].astype(F32) * (u_ref[rs, cs].astype(F32) * mixed))
            merged_scr[rs, cs] = merged.astype(BF16)
    z = jnp.dot(merged_scr[...], wo_ref[...], preferred_element_type=F32)
    o_ref[...] = h_ref[...] + _rms_rows(z, gpost_ref[...])


def _mix_out(sv, u, oa_prompt, oa_sample, gates, w_mix, b_mix, w_o, g_post, h):
    rows = sv.shape[0]
    prompt_blocks = oa_prompt.shape[0] // TM_MIX
    blk = pl.BlockSpec((TM_MIX, D_MODEL), lambda i: (i, 0))
    kind = lambda i: jnp.where(i >= prompt_blocks, 1, 0)
    return pl.pallas_call(
        functools.partial(_mix_out_kernel, prompt_blocks=prompt_blocks),
        grid=(rows // TM_MIX,),
        in_specs=[
            blk, blk,
            pl.BlockSpec((TM_MIX, D_MODEL), lambda i: (jnp.minimum(i, prompt_blocks - 1), 0)),
            pl.BlockSpec((TM_MIX, D_MODEL), lambda i: (jnp.maximum(i - prompt_blocks, 0), 0)),
            pl.BlockSpec((TM_MIX, D_MODEL), lambda i: (i, 0)),
            pl.BlockSpec((TM_MIX, D_MODEL), lambda i: (i, 1)),
            pl.BlockSpec((1, GROUPS, CHUNK, CHUNK), lambda i: (kind(i), 0, 0, 0)),
            pl.BlockSpec((1, CHUNK, D_MODEL), lambda i: (kind(i), 0, 0)),
            pl.BlockSpec((D_MODEL, D_MODEL), lambda i: (0, 0), pipeline_mode=pl.Buffered(1)),
            pl.BlockSpec((1, D_MODEL), lambda i: (0, 0)),
            blk,
        ],
        out_specs=blk,
        out_shape=jax.ShapeDtypeStruct((rows, D_MODEL), F32),
        scratch_shapes=[pltpu.VMEM((TM_MIX, D_MODEL), BF16)],
        compiler_params=pltpu.CompilerParams(
            dimension_semantics=("parallel",),
            vmem_limit_bytes=VMEM_LIMIT),
        name="mix_out_proj",
    )(sv, u, oa_prompt, oa_sample, gates, gates, w_mix, b_mix, w_o, g_post, h)


def kernel(x_prompt, x_sample, cache_k, cache_v, page_table, rel_bias, ffn1_norm_pre, ffn1_w_gate, ffn1_w_up, ffn1_w_down, ffn1_norm_post, mix_norm_pre, w_in, lambda_q1, lambda_k1, lambda_q2, lambda_k2, attn_head_norm, gmlp_ln_g, gmlp_ln_b, gmlp_w_s, gmlp_b_s, w_o, mix_norm_post, ffn2_norm_pre, ffn2_w_gate, ffn2_w_up, ffn2_w_down, ffn2_norm_post):
    batch, seq, _ = x_prompt.shape
    n_seq, t_new, _ = x_sample.shape
    depth = cache_k.shape[0]
    assert depth == 1
    rows_p = batch * seq
    rows_s = n_seq * t_new
    rows = rows_p + rows_s
    assert t_new * 2 * N_HEADS == ROWS_S and CHUNK % t_new == 0

    row = lambda a: a.reshape(1, -1).astype(F32)
    bf = lambda a: a[0].astype(BF16)

    h1, n1 = _ffn(x_prompt.reshape(rows_p, D_MODEL), row(ffn1_norm_pre), bf(ffn1_w_gate),
                  bf(ffn1_w_up), bf(ffn1_w_down), row(ffn1_norm_post), row(mix_norm_pre),
                  row0=0, rows=rows, emit_next=True, x2=x_sample.reshape(rows_s, D_MODEL))

    w_in_b = bf(w_in)
    (q_p,) = _proj(n1, w_in_b, sec0=0, nsec=1, row0=0, rows=rows_p, mode="q_bf16")
    (q_s,) = _proj(n1, w_in_b, sec0=0, nsec=1, row0=rows_p, rows=rows_s, mode="q_f32")
    k_p, k_pb = _proj(n1, w_in_b, sec0=1, nsec=1, row0=0, rows=rows_p, mode="kv")
    k_s, _ = _proj(n1, w_in_b, sec0=1, nsec=1, row0=rows_p, rows=rows_s, mode="kv")
    v_p, v_pb = _proj(n1, w_in_b, sec0=2, nsec=1, row0=0, rows=rows_p, mode="kv")
    v_s, _ = _proj(n1, w_in_b, sec0=2, nsec=1, row0=rows_p, rows=rows_s, mode="kv")
    (u,) = _proj(n1, w_in_b, sec0=3, nsec=1, row0=0, rows=rows, mode="gelu")
    (sv,) = _proj(n1, w_in_b, sec0=4, nsec=1, row0=0, rows=rows, mode="gelu_ln",
                  extra=(row(gmlp_ln_g), row(gmlp_ln_b)))
    (gates,) = _proj(n1, w_in_b, sec0=5, nsec=2, row0=0, rows=rows, mode="sigmoid")

    tbl = _bias_by_distance(rel_bias, SUB + 1)
    g = ((tbl - rel_bias[N_BUCKETS - 1][None, :].astype(F32)) * LOG2E).T
    vec = jnp.concatenate([g[:, ::-1], jnp.zeros((N_HEADS, SUB), F32)], axis=1)
    a = jnp.tile(vec, (1, SUB))[:, :SUB * 2 * SUB].reshape(N_HEADS, SUB, 2 * SUB)
    ii = jnp.arange(SUB)
    d1 = a[:, :, :SUB]
    d0 = jnp.where(ii[:, None] >= ii[None, :], a[:, :, SUB:], -jnp.inf)
    dtiles = jnp.stack([d0, d1], axis=1)

    def for_sample(b):
        n = b.shape[-1]
        bt = jnp.transpose(b, (2, 0, 1))[:, None, :, None, :]
        return jnp.broadcast_to(bt, (n, N_HEADS, N_HEADS, 2, t_new)).reshape(n * N_HEADS, ROWS_S)

    bias_last = for_sample(d1[:, :t_new, :])
    bias_new = for_sample(d0[:, :t_new, :t_new])

    lams = (row(lambda_q1), row(lambda_k1), row(lambda_q2), row(lambda_k2))
    gh = row(attn_head_norm)

    oa_p = _attn_prompt(q_p, k_pb, v_pb, jnp.swapaxes(dtiles, 2, 3), lams, gh, batch=batch, seq=seq)
    width = N_HEADS * D_VHEAD
    oa_s = _attn_sample(page_table,
                        q_s.reshape(n_seq, t_new, width),
                        cache_k.reshape(-1, D_VHEAD),
                        cache_v.reshape(-1, D_VHEAD),
                        k_s.reshape(-1, D_VHEAD), v_s.reshape(-1, D_VHEAD),
                        bias_last, bias_new, lams, gh)

    w_s = gmlp_w_s[0].astype(F32)
    w_prompt = jnp.tril(w_s)
    w_small = jnp.tril(w_s[:, :t_new, :t_new])
    eye = jnp.eye(CHUNK // t_new, dtype=F32)
    w_sample = jnp.einsum("ab,gts->gatbs", eye, w_small).reshape(GROUPS, CHUNK, CHUNK)
    w_mix = jnp.stack([w_prompt, w_sample]).astype(BF16)
    b_s = gmlp_b_s[0].astype(F32)
    b_prompt = jnp.repeat(b_s.T, GDIM, axis=1)
    b_sample = jnp.tile(jnp.repeat(b_s[:, :t_new].T, GDIM, axis=1), (CHUNK // t_new, 1))
    b_mix = jnp.stack([b_prompt, b_sample])

    h2 = _mix_out(sv, u, oa_p, oa_s.reshape(rows_s, width), gates, w_mix, b_mix,
                  bf(w_o), row(mix_norm_post), h1)

    f2 = (row(ffn2_norm_pre), bf(ffn2_w_gate), bf(ffn2_w_up), bf(ffn2_w_down),
          row(ffn2_norm_post), row(ffn2_norm_post))
    (y_p,) = _ffn(h2, *f2, row0=0, rows=rows_p, emit_next=False)
    (y_s,) = _ffn(h2, *f2, row0=rows_p, rows=rows_s, emit_next=False)

    return (y_p.reshape(batch, seq, D_MODEL),
            y_s.reshape(n_seq, t_new, D_MODEL),
            k_p.reshape(1, batch, seq, N_HEADS, 2 * D_HEAD),
            v_p.reshape(1, batch, seq, N_HEADS, D_VHEAD),
            k_s.reshape(1, n_seq, t_new, N_HEADS, 2 * D_HEAD),
            v_s.reshape(1, n_seq, t_new, N_HEADS, D_VHEAD),
            sv[rows_p:].reshape(1, n_seq, t_new, D_MODEL))
```

```python
import functools
import math

import jax
import jax.numpy as jnp
from jax import lax
from jax.experimental import pallas as pl
from jax.experimental.pallas import tpu as pltpu

F32 = jnp.float32
BF16 = jnp.bfloat16

D_MODEL = 2048
D_FF = 5632
N_HEADS = 8
D_HEAD = 128
D_VHEAD = 256
SEC_W = 2048
N_BUCKETS = 32
MAX_DISTANCE = 128
PAGE = 128
CHUNK = 128
GROUPS = 16
GDIM = 128
EPS = 1e-6
SCALE = D_HEAD ** -0.5
LOG2E = math.log2(math.e)
QSCALE = SCALE * LOG2E
LAM_INIT = 0.8 - 0.6 * math.exp(-0.3 * 0)

VMEM_LIMIT = 56 * 1024 * 1024
VMEM_LIMIT_FFN = 60 * 1024 * 1024

TM_FFN = 1024
FFN_SPLIT = 2
TF_FFN = 512
TM_PROJ = 1024
PROJ_SPLIT = 4
T_ATT = 512


def _rms_rows(x, g):
    return x * lax.rsqrt(jnp.mean(x * x, axis=-1, keepdims=True) + EPS) * g


def _ffn_kernel(*refs, emit_next, n_first):
    if n_first is None:
        x_ref, *refs = refs
        x2_ref = None
    else:
        x_ref, x2_ref, *refs = refs
    gpre_ref, wg_ref, wu_ref, wd_ref, gpost_ref, gnext_ref, h_ref, xn_ref = refs[:8]
    n_ref = xn_ref if emit_next else None
    j = pl.program_id(1)

    def with_x(cond, fn):
        if x2_ref is None:
            pl.when(cond)(lambda: fn(x_ref))
        else:
            first = pl.program_id(0) < n_first
            pl.when(cond & first)(lambda: fn(x_ref))
            pl.when(cond & jnp.logical_not(first))(lambda: fn(x2_ref))

    def prologue(xr):
        xn_ref[...] = _rms_rows(xr[...], gpre_ref[...]).astype(BF16)
        h_ref[...] = jnp.zeros(h_ref.shape, F32)

    with_x(j == 0, prologue)

    wg, wu, wd = wg_ref[...], wu_ref[...], wd_ref[...]
    sub = TM_FFN // FFN_SPLIT
    for r in range(FFN_SPLIT):
        rs = slice(r * sub, (r + 1) * sub)
        xn = xn_ref[rs]
        g = jnp.dot(xn, wg, preferred_element_type=F32)
        u = jnp.dot(xn, wu, preferred_element_type=F32)
        a = (g * jax.nn.sigmoid(g) * u).astype(BF16)
        h_ref[rs] += jnp.dot(a, wd, preferred_element_type=F32)

    def epilogue(xr):
        h = xr[...] + 0.5 * _rms_rows(h_ref[...], gpost_ref[...])
        h_ref[...] = h
        if emit_next:
            n_ref[...] = _rms_rows(h, gnext_ref[...]).astype(BF16)

    with_x(j == pl.num_programs(1) - 1, epilogue)


def _ffn(x, gpre, wg, wu, wd, gpost, gnext, *, row0, rows, emit_next, x2=None):
    if x2 is not None:
        assert row0 == 0 and rows == x.shape[0] + x2.shape[0]
    nb = rows // TM_FFN
    b0 = row0 // TM_FFN
    nf = D_FF // TF_FFN
    n_first = None if x2 is None else x.shape[0] // TM_FFN
    vec = pl.BlockSpec((1, D_MODEL), lambda i, j: (0, 0))
    row_blk = pl.BlockSpec((TM_FFN, D_MODEL), lambda i, j: (i, 0))
    if x2 is None:
        x_specs = [pl.BlockSpec((TM_FFN, D_MODEL), lambda i, j: (i + b0, 0),
                                pipeline_mode=pl.Buffered(1))]
        xs = [x]
    else:
        x_specs = [pl.BlockSpec((TM_FFN, D_MODEL), lambda i, j: (jnp.minimum(i, n_first - 1), 0),
                                pipeline_mode=pl.Buffered(1)),
                   pl.BlockSpec((TM_FFN, D_MODEL), lambda i, j: (jnp.maximum(i - n_first, 0), 0),
                                pipeline_mode=pl.Buffered(1))]
        xs = [x, x2]
    out_shape = [jax.ShapeDtypeStruct((rows, D_MODEL), F32)]
    out_specs = [row_blk]
    scratch = []
    if emit_next:
        out_shape.append(jax.ShapeDtypeStruct((rows, D_MODEL), BF16))
        out_specs.append(row_blk)
    else:
        scratch.append(pltpu.VMEM((TM_FFN, D_MODEL), BF16))
    res = pl.pallas_call(
        functools.partial(_ffn_kernel, emit_next=emit_next, n_first=n_first),
        grid=(nb, nf),
        in_specs=x_specs + [
            vec,
            pl.BlockSpec((D_MODEL, TF_FFN), lambda i, j: (0, j)),
            pl.BlockSpec((D_MODEL, TF_FFN), lambda i, j: (0, j)),
            pl.BlockSpec((TF_FFN, D_MODEL), lambda i, j: (j, 0)),
            vec,
            vec,
        ],
        out_specs=out_specs,
        out_shape=out_shape,
        scratch_shapes=scratch,
        compiler_params=pltpu.CompilerParams(
            dimension_semantics=("parallel", "arbitrary"),
            vmem_limit_bytes=VMEM_LIMIT_FFN),
        name="ffn_emit_next" if emit_next else "ffn",
    )(*xs, gpre, wg, wu, wd, gpost, gnext)
    return res


def _gelu(x):
    return 0.5 * x * (1.0 + lax.erf(x * (1.0 / math.sqrt(2.0))))


def _proj_kernel(n_ref, w_ref, *rest, mode):
    w = w_ref[...]
    sub = TM_PROJ // PROJ_SPLIT
    for r in range(PROJ_SPLIT):
        rs = slice(r * sub, (r + 1) * sub)
        z = jnp.dot(n_ref[rs], w, preferred_element_type=F32)
        if mode == "q_bf16":
            (o_ref,) = rest
            o_ref[rs] = (z * QSCALE).astype(BF16)
        elif mode == "q_f32":
            (o_ref,) = rest
            o_ref[rs] = z * QSCALE
        elif mode == "kv":
            o_ref, ob_ref = rest
            for h in range(N_HEADS):
                o_ref[rs, h, :] = z[:, h * D_VHEAD:(h + 1) * D_VHEAD]
            ob_ref[rs] = z.astype(BF16)
        elif mode == "gelu":
            (o_ref,) = rest
            o_ref[rs] = _gelu(z).astype(BF16)
        elif mode == "gelu_ln":
            g_ref, b_ref, o_ref = rest
            a = _gelu(z)
            mu = jnp.mean(a, axis=-1, keepdims=True)
            ac = a - mu
            y = ac * lax.rsqrt(jnp.mean(ac * ac, axis=-1, keepdims=True) + EPS)
            o_ref[rs] = y * g_ref[...] + b_ref[...]
        elif mode == "sigmoid":
            (o_ref,) = rest
            o_ref[rs] = jax.nn.sigmoid(z).astype(BF16)
        else:
            raise ValueError(mode)


def _proj(n, w_in, *, sec0, nsec, row0, rows, mode, extra=()):
    nb = rows // TM_PROJ
    b0 = row0 // TM_PROJ
    blk = pl.BlockSpec((TM_PROJ, SEC_W), lambda s, i: (i, s))
    if mode in ("q_bf16", "gelu", "sigmoid"):
        out_dtypes = [BF16]
    elif mode == "kv":
        out_dtypes = [F32, BF16]
    else:
        out_dtypes = [F32]
    out_shape = [jax.ShapeDtypeStruct((rows, nsec * SEC_W), dt) for dt in out_dtypes]
    out_specs = [blk] * len(out_dtypes)
    if mode == "kv":
        assert nsec == 1
        out_shape[0] = jax.ShapeDtypeStruct((rows, N_HEADS, D_VHEAD), F32)
        out_specs[0] = pl.BlockSpec((TM_PROJ, N_HEADS, D_VHEAD), lambda s, i: (i, 0, 0))
    vec = pl.BlockSpec((1, SEC_W), lambda s, i: (0, 0))
    res = pl.pallas_call(
        functools.partial(_proj_kernel, mode=mode),
        grid=(nsec, nb),
        in_specs=[
            pl.BlockSpec((TM_PROJ, D_MODEL), lambda s, i: (i + b0, 0)),
            pl.BlockSpec((D_MODEL, SEC_W), lambda s, i: (0, s + sec0)),
        ] + [vec] * len(extra),
        out_specs=out_specs,
        out_shape=out_shape,
        compiler_params=pltpu.CompilerParams(
            dimension_semantics=("parallel", "parallel"),
            vmem_limit_bytes=VMEM_LIMIT),
        name="proj_" + mode,
    )(n, w_in, *extra)
    return res


def _lambda_full(lq1_ref, lk1_ref, lq2_ref, lk2_ref):
    s1 = jnp.sum(lq1_ref[...] * lk1_ref[...], axis=-1, keepdims=True)
    s2 = jnp.sum(lq2_ref[...] * lk2_ref[...], axis=-1, keepdims=True)
    return jnp.exp(s1) - jnp.exp(s2) + LAM_INIT


def _head_norm(o, gh):
    return _rms_rows(o, gh) * (1.0 - LAM_INIT)


def _bias_by_distance(rel_bias, nmax):
    n = jnp.arange(nmax)
    max_exact = N_BUCKETS // 2
    nf = jnp.maximum(n, 1).astype(F32)
    large = max_exact + (jnp.log(nf / max_exact) / math.log(MAX_DISTANCE / max_exact)
                         * (N_BUCKETS - max_exact)).astype(jnp.int32)
    large = jnp.minimum(large, N_BUCKETS - 1)
    bucket = jnp.where(n < max_exact, n, large)
    return rel_bias[bucket].astype(F32)


SUB = 128
Q_SPLIT = 2
HEADS_PER_STEP = 8


def _attn_prompt_kernel(qi_tab, ki_tab, q_ref, k_ref, v_ref, dt_ref,
                        lq1_ref, lk1_ref, lq2_ref, lk2_ref, gh_ref,
                        o_ref, m_scr, l_scr, acc_scr, bias_scr):
    t = pl.program_id(2)
    qi = qi_tab[t]
    ki = ki_tab[t]
    d = qi - ki

    @pl.when(t == 0)
    def _():
        neg = jnp.full((SUB, SUB), -jnp.inf, F32)
        zero = jnp.zeros((SUB, SUB), F32)
        nsub = T_ATT // SUB
        for hh in range(HEADS_PER_STEP):
            d0 = dt_ref[hh, 0]
            d1 = dt_ref[hh, 1]
            for a in range(nsub):
                for b in range(nsub):
                    ks, qs = slice(b * SUB, (b + 1) * SUB), slice(a * SUB, (a + 1) * SUB)
                    bias_scr[hh, 0, ks, qs] = (d0 if a == b else d1 if a == b + 1
                                               else neg if a < b else zero)
                    bias_scr[hh, 1, ks, qs] = d1 if (a == 0 and b == nsub - 1) else zero

    @pl.when(ki == 0)
    def _():
        m_scr[...] = jnp.full(m_scr.shape, -jnp.inf, F32)
        l_scr[...] = jnp.zeros(l_scr.shape, F32)
        acc_scr[...] = jnp.zeros(acc_scr.shape, F32)

    nt = (((1,), (1,)), ((), ()))
    tn = (((0,), (0,)), ((), ()))
    qw = T_ATT // Q_SPLIT

    def update(hh, mi, qh, near):
        c0 = hh * 2 * D_HEAD + mi * D_HEAD
        qs = slice(qh * qw, (qh + 1) * qw)
        kr = slice(0, (qh + 1) * qw if near == 0 else T_ATT)
        s = lax.dot_general(k_ref[kr, c0:c0 + D_HEAD], q_ref[qs, c0:c0 + D_HEAD], nt,
                            preferred_element_type=F32)
        if near is not None:
            s = s + bias_scr[hh, near, kr, qs]
        m_old = m_scr[hh, mi, :, qs]
        m_new = jnp.maximum(m_old, jnp.max(s, axis=0, keepdims=True))
        alpha = jnp.exp2(m_old - m_new)
        p = jnp.exp2(s - m_new)
        l_scr[hh, mi, :, qs] = alpha * l_scr[hh, mi, :, qs] + jnp.sum(p, axis=0, keepdims=True)
        acc_scr[hh, mi, :, qs] = alpha * acc_scr[hh, mi, :, qs] + lax.dot_general(
            v_ref[kr, hh * D_VHEAD:(hh + 1) * D_VHEAD], p.astype(BF16), tn,
            preferred_element_type=F32)
        m_scr[hh, mi, :, qs] = m_new

    def sweep(near):
        for hh in range(HEADS_PER_STEP):
            for qh in range(Q_SPLIT):
                for mi in range(2):
                    update(hh, mi, qh, near)

    pl.when(d >= 2)(lambda: sweep(None))
    pl.when(d == 1)(lambda: sweep(1))

    @pl.when(d == 0)
    def _():
        sweep(0)
        lam = _lambda_full(lq1_ref, lk1_ref, lq2_ref, lk2_ref)
        for hh in range(HEADS_PER_STEP):
            ot = (acc_scr[hh, 0] / l_scr[hh, 0]
                  - lam * (acc_scr[hh, 1] / l_scr[hh, 1]))
            ot = ot * lax.rsqrt(jnp.mean(ot * ot, axis=0, keepdims=True) + EPS)
            o_ref[:, hh * D_VHEAD:(hh + 1) * D_VHEAD] = (
                jnp.transpose(ot) * (gh_ref[...] * (1.0 - LAM_INIT))).astype(BF16)


def _attn_prompt(q, k, v, dtiles, lams, gh, *, batch, seq):
    nq = seq // T_ATT
    hps = HEADS_PER_STEP
    tri = [(qi, ki) for qi in range(nq) for ki in range(qi + 1)]
    qi_tab = jnp.asarray([a for a, _ in tri], jnp.int32)
    ki_tab = jnp.asarray([b for _, b in tri], jnp.int32)
    vec = pl.BlockSpec((1, D_HEAD), lambda b, h, t, qt, kt: (0, 0))
    grid_spec = pltpu.PrefetchScalarGridSpec(
        num_scalar_prefetch=2,
        grid=(batch, N_HEADS // hps, len(tri)),
        in_specs=[
            pl.BlockSpec((T_ATT, hps * 2 * D_HEAD), lambda b, h, t, qt, kt: (b * nq + qt[t], h)),
            pl.BlockSpec((T_ATT, hps * 2 * D_HEAD), lambda b, h, t, qt, kt: (b * nq + kt[t], h)),
            pl.BlockSpec((T_ATT, hps * D_VHEAD), lambda b, h, t, qt, kt: (b * nq + kt[t], h)),
            pl.BlockSpec((hps, 2, SUB, SUB), lambda b, h, t, qt, kt: (h, 0, 0, 0)),
            vec, vec, vec, vec,
            pl.BlockSpec((1, D_VHEAD), lambda b, h, t, qt, kt: (0, 0)),
        ],
        out_specs=pl.BlockSpec((T_ATT, hps * D_VHEAD), lambda b, h, t, qt, kt: (b * nq + qt[t], h)),
        scratch_shapes=[pltpu.VMEM((hps, 2, 1, T_ATT), F32),
                        pltpu.VMEM((hps, 2, 1, T_ATT), F32),
                        pltpu.VMEM((hps, 2, D_VHEAD, T_ATT), F32),
                        pltpu.VMEM((hps, 2, T_ATT, T_ATT), F32)],
    )
    return pl.pallas_call(
        _attn_prompt_kernel,
        grid_spec=grid_spec,
        out_shape=jax.ShapeDtypeStruct((batch * seq, N_HEADS * D_VHEAD), BF16),
        compiler_params=pltpu.CompilerParams(
            dimension_semantics=("parallel", "parallel", "arbitrary"),
            vmem_limit_bytes=VMEM_LIMIT),
        name="attn_prompt",
    )(qi_tab, ki_tab, q, k, v, dtiles, *lams, gh)


ROWS_S = 128
PAGES_PER_STEP = 8


def _attn_sample_kernel(pt_ref, q_ref, *refs, n_steps, t_new):
    pps = PAGES_PER_STEP
    page_refs = refs[:2 * pps]
    (kn_ref, vn_ref, blast_ref, bnew_ref,
     lq1_ref, lk1_ref, lq2_ref, lk2_ref, gh_ref,
     o_ref, qt_scr, m_scr, l_scr, acc_scr) = refs[2 * pps:]
    p = pl.program_id(1)
    rph = 2 * t_new
    tn = (((0,), (0,)), ((), ()))

    def to_rows(x):
        return jnp.transpose(jnp.broadcast_to(x, (8, ROWS_S)))[:, :1]

    @pl.when(p == 0)
    def _():
        q = q_ref[0]
        zero = jnp.zeros((t_new, D_HEAD), F32)
        blocks = []
        for h in range(N_HEADS):
            q1 = q[:, h * 2 * D_HEAD:h * 2 * D_HEAD + D_HEAD]
            q2 = q[:, h * 2 * D_HEAD + D_HEAD:(h + 1) * 2 * D_HEAD]
            blocks += [jnp.concatenate([q1, zero], axis=1), jnp.concatenate([zero, q2], axis=1)]
        qt_scr[...] = jnp.transpose(jnp.concatenate(blocks, axis=0)).astype(BF16)
        m_scr[...] = jnp.full(m_scr.shape, -jnp.inf, F32)
        l_scr[...] = jnp.zeros(l_scr.shape, F32)
        acc_scr[...] = jnp.zeros(acc_scr.shape, F32)

    sub = lax.broadcasted_iota(jnp.int32, (N_HEADS, ROWS_S), 0)
    lane = lax.broadcasted_iota(jnp.int32, (N_HEADS, ROWS_S), 1)
    head_mask = jnp.where(sub == lane // rph, 0.0, -jnp.inf).astype(F32)

    def scores(k):
        st = jnp.dot(k, qt_scr[...], preferred_element_type=F32)
        n = st.shape[0]
        return (st.reshape(n // N_HEADS, N_HEADS, ROWS_S) + head_mask[None]).reshape(n, ROWS_S)

    def attend(sts, vs):
        m_old = m_scr[...]
        m_new = m_old
        for st in sts:
            m_new = jnp.maximum(m_new, jnp.max(st, axis=0, keepdims=True))
        alpha = jnp.exp2(m_old - m_new)
        l_new = alpha * l_scr[...]
        pv = None
        for st, v in zip(sts, vs):
            pr = jnp.exp2(st - m_new)
            l_new = l_new + jnp.sum(pr, axis=0, keepdims=True)
            part = lax.dot_general(pr.astype(BF16), v, tn, preferred_element_type=F32)
            pv = part if pv is None else pv + part
        m_scr[...] = m_new
        l_scr[...] = l_new
        acc_scr[...] = to_rows(alpha) * acc_scr[...] + pv

    sts = [scores(page_refs[2 * r][...].astype(BF16)) for r in range(pps)]
    sts[-1] = sts[-1] + jnp.where(p == n_steps - 1, blast_ref[...], 0.0)
    attend(sts, [page_refs[2 * r + 1][...].astype(BF16) for r in range(pps)])

    @pl.when(p == n_steps - 1)
    def _():
        attend([scores(kn_ref[...].astype(BF16)) + bnew_ref[...]], [vn_ref[...].astype(BF16)])
        lam = _lambda_full(lq1_ref, lk1_ref, lq2_ref, lk2_ref)
        a_all = acc_scr[...] / to_rows(l_scr[...])
        for h in range(N_HEADS):
            a = a_all[h * rph:(h + 1) * rph]
            o = a[:t_new] - lam * a[t_new:]
            o_ref[0, :, h * D_VHEAD:(h + 1) * D_VHEAD] = _head_norm(o, gh_ref[...])


def _attn_sample(page_table, q, cache_k, cache_v, k_new, v_new, bias_last, bias_new, lams, gh):
    n_seq, t_new, width = q.shape
    pps = PAGES_PER_STEP
    n_steps = page_table.shape[1] // pps
    assert n_steps * pps == page_table.shape[1]
    seq_blk = pl.BlockSpec((1, t_new, width), lambda s, p, pt: (s, 0, 0))
    vec = pl.BlockSpec((1, D_HEAD), lambda s, p, pt: (0, 0))
    const = lambda shape: pl.BlockSpec(shape, lambda s, p, pt: (0, 0))

    page_specs, page_args = [], []
    for r in range(pps):
        for arr in (cache_k, cache_v):
            page_specs.append(pl.BlockSpec((PAGE * N_HEADS, D_VHEAD),
                                           lambda s, p, pt, r=r: (pt[s, p * pps + r], 0)))
            page_args.append(arr)
    new_blk = pl.BlockSpec((t_new * N_HEADS, D_VHEAD), lambda s, p, pt: (s, 0))

    grid_spec = pltpu.PrefetchScalarGridSpec(
        num_scalar_prefetch=1,
        grid=(n_seq, n_steps),
        in_specs=[seq_blk] + page_specs + [
            new_blk, new_blk,
            const((PAGE * N_HEADS, ROWS_S)), const((t_new * N_HEADS, ROWS_S)),
            vec, vec, vec, vec,
            pl.BlockSpec((1, D_VHEAD), lambda s, p, pt: (0, 0)),
        ],
        out_specs=seq_blk,
        scratch_shapes=[pltpu.VMEM((2 * D_HEAD, ROWS_S), BF16),
                        pltpu.VMEM((1, ROWS_S), F32),
                        pltpu.VMEM((1, ROWS_S), F32),
                        pltpu.VMEM((ROWS_S, D_VHEAD), F32)],
    )
    return pl.pallas_call(
        functools.partial(_attn_sample_kernel, n_steps=n_steps, t_new=t_new),
        grid_spec=grid_spec,
        out_shape=jax.ShapeDtypeStruct((n_seq, t_new, width), F32),
        compiler_params=pltpu.CompilerParams(
            dimension_semantics=("parallel", "arbitrary"),
            vmem_limit_bytes=VMEM_LIMIT),
        name="attn_sample",
    )(page_table, q, *page_args, k_new, v_new, bias_last, bias_new, *lams, gh)


TM_MIX = 256


def _mix_out_kernel(sv_ref, u_ref, oap_ref, oas_ref, ga_ref, gm_ref, w_ref, b_ref,
                    wo_ref, gpost_ref, h_ref, o_ref, merged_scr, *, prompt_blocks):
    is_prompt = pl.program_id(0) < prompt_blocks
    for c in range(TM_MIX // CHUNK):
        rs = slice(c * CHUNK, (c + 1) * CHUNK)
        for g in range(GROUPS):
            cs = slice(g * GDIM, (g + 1) * GDIM)
            mixed = jnp.dot(w_ref[0, g], sv_ref[rs, cs].astype(BF16),
                            preferred_element_type=F32) + b_ref[0, :, cs]
            oa = jnp.where(is_prompt, oap_ref[rs, cs].astype(F32), oas_ref[rs, cs])
            merged = (ga_ref[rs, cs].astype(F32) * oa
                      + gm_ref[rs, cs].astype(F32) * (u_ref[rs, cs].astype(F32) * mixed))
            merged_scr[rs, cs] = merged.astype(BF16)
    z = jnp.dot(merged_scr[...], wo_ref[...], preferred_element_type=F32)
    o_ref[...] = h_ref[...] + _rms_rows(z, gpost_ref[...])


def _mix_out(sv, u, oa_prompt, oa_sample, gates, w_mix, b_mix, w_o, g_post, h):
    rows = sv.shape[0]
    prompt_blocks = oa_prompt.shape[0] // TM_MIX
    blk = pl.BlockSpec((TM_MIX, D_MODEL), lambda i: (i, 0))
    kind = lambda i: jnp.where(i >= prompt_blocks, 1, 0)
    return pl.pallas_call(
        functools.partial(_mix_out_kernel, prompt_blocks=prompt_blocks),
        grid=(rows // TM_MIX,),
        in_specs=[
            blk, blk,
            pl.BlockSpec((TM_MIX, D_MODEL), lambda i: (jnp.minimum(i, prompt_blocks - 1), 0)),
            pl.BlockSpec((TM_MIX, D_MODEL), lambda i: (jnp.maximum(i - prompt_blocks, 0), 0)),
            pl.BlockSpec((TM_MIX, D_MODEL), lambda i: (i, 0)),
            pl.BlockSpec((TM_MIX, D_MODEL), lambda i: (i, 1)),
            pl.BlockSpec((1, GROUPS, CHUNK, CHUNK), lambda i: (kind(i), 0, 0, 0)),
            pl.BlockSpec((1, CHUNK, D_MODEL), lambda i: (kind(i), 0, 0)),
            pl.BlockSpec((D_MODEL, D_MODEL), lambda i: (0, 0), pipeline_mode=pl.Buffered(1)),
            pl.BlockSpec((1, D_MODEL), lambda i: (0, 0)),
            blk,
        ],
        out_specs=blk,
        out_shape=jax.ShapeDtypeStruct((rows, D_MODEL), F32),
        scratch_shapes=[pltpu.VMEM((TM_MIX, D_MODEL), BF16)],
        compiler_params=pltpu.CompilerParams(
            dimension_semantics=("parallel",),
            vmem_limit_bytes=VMEM_LIMIT),
        name="mix_out_proj",
    )(sv, u, oa_prompt, oa_sample, gates, gates, w_mix, b_mix, w_o, g_post, h)


def kernel(x_prompt, x_sample, cache_k, cache_v, page_table, rel_bias, ffn1_norm_pre, ffn1_w_gate, ffn1_w_up, ffn1_w_down, ffn1_norm_post, mix_norm_pre, w_in, lambda_q1, lambda_k1, lambda_q2, lambda_k2, attn_head_norm, gmlp_ln_g, gmlp_ln_b, gmlp_w_s, gmlp_b_s, w_o, mix_norm_post, ffn2_norm_pre, ffn2_w_gate, ffn2_w_up, ffn2_w_down, ffn2_norm_post):
    batch, seq, _ = x_prompt.shape
    n_seq, t_new, _ = x_sample.shape
    depth = cache_k.shape[0]
    assert depth == 1
    rows_p = batch * seq
    rows_s = n_seq * t_new
    rows = rows_p + rows_s
    assert t_new * 2 * N_HEADS == ROWS_S and CHUNK % t_new == 0

    row = lambda a: a.reshape(1, -1).astype(F32)
    bf = lambda a: a[0].astype(BF16)

    h1, n1 = _ffn(x_prompt.reshape(rows_p, D_MODEL), row(ffn1_norm_pre), bf(ffn1_w_gate),
                  bf(ffn1_w_up), bf(ffn1_w_down), row(ffn1_norm_post), row(mix_norm_pre),
                  row0=0, rows=rows, emit_next=True, x2=x_sample.reshape(rows_s, D_MODEL))

    w_in_b = bf(w_in)
    (q_p,) = _proj(n1, w_in_b, sec0=0, nsec=1, row0=0, rows=rows_p, mode="q_bf16")
    (q_s,) = _proj(n1, w_in_b, sec0=0, nsec=1, row0=rows_p, rows=rows_s, mode="q_f32")
    k_p, k_pb = _proj(n1, w_in_b, sec0=1, nsec=1, row0=0, rows=rows_p, mode="kv")
    k_s, _ = _proj(n1, w_in_b, sec0=1, nsec=1, row0=rows_p, rows=rows_s, mode="kv")
    v_p, v_pb = _proj(n1, w_in_b, sec0=2, nsec=1, row0=0, rows=rows_p, mode="kv")
    v_s, _ = _proj(n1, w_in_b, sec0=2, nsec=1, row0=rows_p, rows=rows_s, mode="kv")
    (u,) = _proj(n1, w_in_b, sec0=3, nsec=1, row0=0, rows=rows, mode="gelu")
    (sv,) = _proj(n1, w_in_b, sec0=4, nsec=1, row0=0, rows=rows, mode="gelu_ln",
                  extra=(row(gmlp_ln_g), row(gmlp_ln_b)))
    (gates,) = _proj(n1, w_in_b, sec0=5, nsec=2, row0=0, rows=rows, mode="sigmoid")

    tbl = _bias_by_distance(rel_bias, SUB + 1)
    g = ((tbl - rel_bias[N_BUCKETS - 1][None, :].astype(F32)) * LOG2E).T
    vec = jnp.concatenate([g[:, ::-1], jnp.zeros((N_HEADS, SUB), F32)], axis=1)
    a = jnp.tile(vec, (1, SUB))[:, :SUB * 2 * SUB].reshape(N_HEADS, SUB, 2 * SUB)
    ii = jnp.arange(SUB)
    d1 = a[:, :, :SUB]
    d0 = jnp.where(ii[:, None] >= ii[None, :], a[:, :, SUB:], -jnp.inf)
    dtiles = jnp.stack([d0, d1], axis=1)

    def for_sample(b):
        n = b.shape[-1]
        bt = jnp.transpose(b, (2, 0, 1))[:, None, :, None, :]
        return jnp.broadcast_to(bt, (n, N_HEADS, N_HEADS, 2, t_new)).reshape(n * N_HEADS, ROWS_S)

    bias_last = for_sample(d1[:, :t_new, :])
    bias_new = for_sample(d0[:, :t_new, :t_new])

    lams = (row(lambda_q1), row(lambda_k1), row(lambda_q2), row(lambda_k2))
    gh = row(attn_head_norm)

    oa_p = _attn_prompt(q_p, k_pb, v_pb, jnp.swapaxes(dtiles, 2, 3), lams, gh, batch=batch, seq=seq)
    width = N_HEADS * D_VHEAD
    oa_s = _attn_sample(page_table,
                        q_s.reshape(n_seq, t_new, width),
                        cache_k.reshape(-1, D_VHEAD),
                        cache_v.reshape(-1, D_VHEAD),
                        k_s.reshape(-1, D_VHEAD), v_s.reshape(-1, D_VHEAD),
                        bias_last, bias_new, lams, gh)

    w_s = gmlp_w_s[0].astype(F32)
    w_prompt = jnp.tril(w_s)
    w_small = jnp.tril(w_s[:, :t_new, :t_new])
    eye = jnp.eye(CHUNK // t_new, dtype=F32)
    w_sample = jnp.einsum("ab,gts->gatbs", eye, w_small).reshape(GROUPS, CHUNK, CHUNK)
    w_mix = jnp.stack([w_prompt, w_sample]).astype(BF16)
    b_s = gmlp_b_s[0].astype(F32)
    b_prompt = jnp.repeat(b_s.T, GDIM, axis=1)
    b_sample = jnp.tile(jnp.repeat(b_s[:, :t_new].T, GDIM, axis=1), (CHUNK // t_new, 1))
    b_mix = jnp.stack([b_prompt, b_sample])

    h2 = _mix_out(sv, u, oa_p, oa_s.reshape(rows_s, width), gates, w_mix, b_mix,
                  bf(w_o), row(mix_norm_post), h1)

    f2 = (row(ffn2_norm_pre), bf(ffn2_w_gate), bf(ffn2_w_up), bf(ffn2_w_down),
          row(ffn2_norm_post), row(ffn2_norm_post))
    (y_p,) = _ffn(h2, *f2, row0=0, rows=rows_p, emit_next=False)
    (y_s,) = _ffn(h2, *f2, row0=rows_p, rows=rows_s, emit_next=False)

    return (y_p.reshape(batch, seq, D_MODEL),
            y_s.reshape(n_seq, t_new, D_MODEL),
            k_p.reshape(1, batch, seq, N_HEADS, 2 * D_HEAD),
            v_p.reshape(1, batch, seq, N_HEADS, D_VHEAD),
            k_s.reshape(1, n_seq, t_new, N_HEADS, 2 * D_HEAD),
            v_s.reshape(1, n_seq, t_new, N_HEADS, D_VHEAD),
            sv[rows_p:].reshape(1, n_seq, t_new, D_MODEL))
```

```python
import functools
import math

import jax
import jax.numpy as jnp
from jax import lax
from jax.experimental import pallas as pl
from jax.experimental.pallas import tpu as pltpu

F32 = jnp.float32
BF16 = jnp.bfloat16

D_MODEL = 2048
D_FF = 5632
N_HEADS = 8
D_HEAD = 128
D_VHEAD = 256
SEC_W = 2048
N_BUCKETS = 32
MAX_DISTANCE = 128
PAGE = 128
CHUNK = 128
GROUPS = 16
GDIM = 128
EPS = 1e-6
SCALE = D_HEAD ** -0.5
LOG2E = math.log2(math.e)
QSCALE = SCALE * LOG2E
LAM_INIT = 0.8 - 0.6 * math.exp(-0.3 * 0)

VMEM_LIMIT = 56 * 1024 * 1024
VMEM_LIMIT_FFN = 60 * 1024 * 1024

TM_FFN = 1024
FFN_SPLIT = 2
TF_FFN = 512
TM_PROJ = 1024
PROJ_SPLIT = 4
T_ATT = 512


def _rms_rows(x, g):
    return x * lax.rsqrt(jnp.mean(x * x, axis=-1, keepdims=True) + EPS) * g


def _ffn_kernel(*refs, emit_next, n_first):
    if n_first is None:
        x_ref, *refs = refs
        x2_ref = None
    else:
        x_ref, x2_ref, *refs = refs
    gpre_ref, wg_ref, wu_ref, wd_ref, gpost_ref, gnext_ref, h_ref, xn_ref = refs[:8]
    n_ref = xn_ref if emit_next else None
    j = pl.program_id(1)

    def with_x(cond, fn):
        if x2_ref is None:
            pl.when(cond)(lambda: fn(x_ref))
        else:
            first = pl.program_id(0) < n_first
            pl.when(cond & first)(lambda: fn(x_ref))
            pl.when(cond & jnp.logical_not(first))(lambda: fn(x2_ref))

    def prologue(xr):
        xn_ref[...] = _rms_rows(xr[...], gpre_ref[...]).astype(BF16)
        h_ref[...] = jnp.zeros(h_ref.shape, F32)

    with_x(j == 0, prologue)

    wg, wu, wd = wg_ref[...], wu_ref[...], wd_ref[...]
    sub = TM_FFN // FFN_SPLIT
    for r in range(FFN_SPLIT):
        rs = slice(r * sub, (r + 1) * sub)
        xn = xn_ref[rs]
        g = jnp.dot(xn, wg, preferred_element_type=F32)
        u = jnp.dot(xn, wu, preferred_element_type=F32)
        a = (g * jax.nn.sigmoid(g) * u).astype(BF16)
        h_ref[rs] += jnp.dot(a, wd, preferred_element_type=F32)

    def epilogue(xr):
        h = xr[...] + 0.5 * _rms_rows(h_ref[...], gpost_ref[...])
        h_ref[...] = h
        if emit_next:
            n_ref[...] = _rms_rows(h, gnext_ref[...]).astype(BF16)

    with_x(j == pl.num_programs(1) - 1, epilogue)


def _ffn(x, gpre, wg, wu, wd, gpost, gnext, *, row0, rows, emit_next, x2=None):
    if x2 is not None:
        assert row0 == 0 and rows == x.shape[0] + x2.shape[0]
    nb = rows // TM_FFN
    b0 = row0 // TM_FFN
    nf = D_FF // TF_FFN
    n_first = None if x2 is None else x.shape[0] // TM_FFN
    vec = pl.BlockSpec((1, D_MODEL), lambda i, j: (0, 0))
    row_blk = pl.BlockSpec((TM_FFN, D_MODEL), lambda i, j: (i, 0))
    if x2 is None:
        x_specs = [pl.BlockSpec((TM_FFN, D_MODEL), lambda i, j: (i + b0, 0),
                                pipeline_mode=pl.Buffered(1))]
        xs = [x]
    else:
        x_specs = [pl.BlockSpec((TM_FFN, D_MODEL), lambda i, j: (jnp.minimum(i, n_first - 1), 0),
                                pipeline_mode=pl.Buffered(1)),
                   pl.BlockSpec((TM_FFN, D_MODEL), lambda i, j: (jnp.maximum(i - n_first, 0), 0),
                                pipeline_mode=pl.Buffered(1))]
        xs = [x, x2]
    out_shape = [jax.ShapeDtypeStruct((rows, D_MODEL), F32)]
    out_specs = [row_blk]
    scratch = []
    if emit_next:
        out_shape.append(jax.ShapeDtypeStruct((rows, D_MODEL), BF16))
        out_specs.append(row_blk)
    else:
        scratch.append(pltpu.VMEM((TM_FFN, D_MODEL), BF16))
    res = pl.pallas_call(
        functools.partial(_ffn_kernel, emit_next=emit_next, n_first=n_first),
        grid=(nb, nf),
        in_specs=x_specs + [
            vec,
            pl.BlockSpec((D_MODEL, TF_FFN), lambda i, j: (0, j)),
            pl.BlockSpec((D_MODEL, TF_FFN), lambda i, j: (0, j)),
            pl.BlockSpec((TF_FFN, D_MODEL), lambda i, j: (j, 0)),
            vec,
            vec,
        ],
        out_specs=out_specs,
        out_shape=out_shape,
        scratch_shapes=scratch,
        compiler_params=pltpu.CompilerParams(
            dimension_semantics=("parallel", "arbitrary"),
            vmem_limit_bytes=VMEM_LIMIT_FFN),
        name="ffn_emit_next" if emit_next else "ffn",
    )(*xs, gpre, wg, wu, wd, gpost, gnext)
    return res


def _gelu(x):
    return 0.5 * x * (1.0 + lax.erf(x * (1.0 / math.sqrt(2.0))))


def _proj_kernel(n_ref, w_ref, *rest, mode):
    w = w_ref[...]
    sub = TM_PROJ // PROJ_SPLIT
    for r in range(PROJ_SPLIT):
        rs = slice(r * sub, (r + 1) * sub)
        z = jnp.dot(n_ref[rs], w, preferred_element_type=F32)
        if mode == "q_bf16":
            (o_ref,) = rest
            o_ref[rs] = (z * QSCALE).astype(BF16)
        elif mode == "q_f32":
            (o_ref,) = rest
            o_ref[rs] = z * QSCALE
        elif mode == "kv":
            o_ref, ob_ref = rest
            for h in range(N_HEADS):
                o_ref[rs, h, :] = z[:, h * D_VHEAD:(h + 1) * D_VHEAD]
            ob_ref[rs] = z.astype(BF16)
        elif mode == "gelu":
            (o_ref,) = rest
            o_ref[rs] = _gelu(z).astype(BF16)
        elif mode == "gelu_ln":
            g_ref, b_ref, o_ref = rest
            a = _gelu(z)
            mu = jnp.mean(a, axis=-1, keepdims=True)
            ac = a - mu
            y = ac * lax.rsqrt(jnp.mean(ac * ac, axis=-1, keepdims=True) + EPS)
            o_ref[rs] = y * g_ref[...] + b_ref[...]
        elif mode == "sigmoid":
            (o_ref,) = rest
            o_ref[rs] = jax.nn.sigmoid(z).astype(BF16)
        else:
            raise ValueError(mode)


def _proj(n, w_in, *, sec0, nsec, row0, rows, mode, extra=()):
    nb = rows // TM_PROJ
    b0 = row0 // TM_PROJ
    blk = pl.BlockSpec((TM_PROJ, SEC_W), lambda s, i: (i, s))
    if mode in ("q_bf16", "gelu", "sigmoid"):
        out_dtypes = [BF16]
    elif mode == "kv":
        out_dtypes = [F32, BF16]
    else:
        out_dtypes = [F32]
    out_shape = [jax.ShapeDtypeStruct((rows, nsec * SEC_W), dt) for dt in out_dtypes]
    out_specs = [blk] * len(out_dtypes)
    if mode == "kv":
        assert nsec == 1
        out_shape[0] = jax.ShapeDtypeStruct((rows, N_HEADS, D_VHEAD), F32)
        out_specs[0] = pl.BlockSpec((TM_PROJ, N_HEADS, D_VHEAD), lambda s, i: (i, 0, 0))
    vec = pl.BlockSpec((1, SEC_W), lambda s, i: (0, 0))
    res = pl.pallas_call(
        functools.partial(_proj_kernel, mode=mode),
        grid=(nsec, nb),
        in_specs=[
            pl.BlockSpec((TM_PROJ, D_MODEL), lambda s, i: (i + b0, 0)),
            pl.BlockSpec((D_MODEL, SEC_W), lambda s, i: (0, s + sec0)),
        ] + [vec] * len(extra),
        out_specs=out_specs,
        out_shape=out_shape,
        compiler_params=pltpu.CompilerParams(
            dimension_semantics=("parallel", "parallel"),
            vmem_limit_bytes=VMEM_LIMIT),
        name="proj_" + mode,
    )(n, w_in, *extra)
    return res


def _lambda_full(lq1_ref, lk1_ref, lq2_ref, lk2_ref):
    s1 = jnp.sum(lq1_ref[...] * lk1_ref[...], axis=-1, keepdims=True)
    s2 = jnp.sum(lq2_ref[...] * lk2_ref[...], axis=-1, keepdims=True)
    return jnp.exp(s1) - jnp.exp(s2) + LAM_INIT


def _head_norm(o, gh):
    return _rms_rows(o, gh) * (1.0 - LAM_INIT)


def _bias_by_distance(rel_bias, nmax):
    n = jnp.arange(nmax)
    max_exact = N_BUCKETS // 2
    nf = jnp.maximum(n, 1).astype(F32)
    large = max_exact + (jnp.log(nf / max_exact) / math.log(MAX_DISTANCE / max_exact)
                         * (N_BUCKETS - max_exact)).astype(jnp.int32)
    large = jnp.minimum(large, N_BUCKETS - 1)
    bucket = jnp.where(n < max_exact, n, large)
    return rel_bias[bucket].astype(F32)


SUB = 128
Q_SPLIT = 2
HEADS_PER_STEP = 8


def _attn_prompt_kernel(qi_tab, ki_tab, q_ref, k_ref, v_ref, dt_ref,
                        lq1_ref, lk1_ref, lq2_ref, lk2_ref, gh_ref,
                        o_ref, m_scr, l_scr, acc_scr, bias_scr):
    t = pl.program_id(2)
    qi = qi_tab[t]
    ki = ki_tab[t]
    d = qi - ki

    @pl.when(t == 0)
    def _():
        neg = jnp.full((SUB, SUB), -jnp.inf, F32)
        zero = jnp.zeros((SUB, SUB), F32)
        nsub = T_ATT // SUB
        for hh in range(HEADS_PER_STEP):
            d0 = dt_ref[hh, 0]
            d1 = dt_ref[hh, 1]
            for a in range(nsub):
                for b in range(nsub):
                    ks, qs = slice(b * SUB, (b + 1) * SUB), slice(a * SUB, (a + 1) * SUB)
                    bias_scr[hh, 0, ks, qs] = (d0 if a == b else d1 if a == b + 1
                                               else neg if a < b else zero)
                    bias_scr[hh, 1, ks, qs] = d1 if (a == 0 and b == nsub - 1) else zero

    @pl.when(ki == 0)
    def _():
        m_scr[...] = jnp.full(m_scr.shape, -jnp.inf, F32)
        l_scr[...] = jnp.zeros(l_scr.shape, F32)
        acc_scr[...] = jnp.zeros(acc_scr.shape, F32)

    nt = (((1,), (1,)), ((), ()))
    tn = (((0,), (0,)), ((), ()))
    qw = T_ATT // Q_SPLIT

    def update(hh, mi, qh, near):
        c0 = hh * 2 * D_HEAD + mi * D_HEAD
        qs = slice(qh * qw, (qh + 1) * qw)
        kr = slice(0, (qh + 1) * qw if near == 0 else T_ATT)
        s = lax.dot_general(k_ref[kr, c0:c0 + D_HEAD], q_ref[qs, c0:c0 + D_HEAD], nt,
                            preferred_element_type=F32)
        if near is not None:
            s = s + bias_scr[hh, near, kr, qs]
        m_old = m_scr[hh, mi, :, qs]
        m_new = jnp.maximum(m_old, jnp.max(s, axis=0, keepdims=True))
        alpha = jnp.exp2(m_old - m_new)
        p = jnp.exp2(s - m_new)
        l_scr[hh, mi, :, qs] = alpha * l_scr[hh, mi, :, qs] + jnp.sum(p, axis=0, keepdims=True)
        acc_scr[hh, mi, :, qs] = alpha * acc_scr[hh, mi, :, qs] + lax.dot_general(
            v_ref[kr, hh * D_VHEAD:(hh + 1) * D_VHEAD], p.astype(BF16), tn,
            preferred_element_type=F32)
        m_scr[hh, mi, :, qs] = m_new

    def sweep(near):
        for hh in range(HEADS_PER_STEP):
            for qh in range(Q_SPLIT):
                for mi in range(2):
                    update(hh, mi, qh, near)

    pl.when(d >= 2)(lambda: sweep(None))
    pl.when(d == 1)(lambda: sweep(1))

    @pl.when(d == 0)
    def _():
        sweep(0)
        lam = _lambda_full(lq1_ref, lk1_ref, lq2_ref, lk2_ref)
        for hh in range(HEADS_PER_STEP):
            ot = (acc_scr[hh, 0] / l_scr[hh, 0]
                  - lam * (acc_scr[hh, 1] / l_scr[hh, 1]))
            ot = ot * lax.rsqrt(jnp.mean(ot * ot, axis=0, keepdims=True) + EPS)
            o_ref[:, hh * D_VHEAD:(hh + 1) * D_VHEAD] = (
                jnp.transpose(ot) * (gh_ref[...] * (1.0 - LAM_INIT))).astype(BF16)


def _attn_prompt(q, k, v, dtiles, lams, gh, *, batch, seq):
    nq = seq // T_ATT
    hps = HEADS_PER_STEP
    tri = [(qi, ki) for qi in range(nq) for ki in range(qi + 1)]
    qi_tab = jnp.asarray([a for a, _ in tri], jnp.int32)
    ki_tab = jnp.asarray([b for _, b in tri], jnp.int32)
    vec = pl.BlockSpec((1, D_HEAD), lambda b, h, t, qt, kt: (0, 0))
    grid_spec = pltpu.PrefetchScalarGridSpec(
        num_scalar_prefetch=2,
        grid=(batch, N_HEADS // hps, len(tri)),
        in_specs=[
            pl.BlockSpec((T_ATT, hps * 2 * D_HEAD), lambda b, h, t, qt, kt: (b * nq + qt[t], h)),
            pl.BlockSpec((T_ATT, hps * 2 * D_HEAD), lambda b, h, t, qt, kt: (b * nq + kt[t], h)),
            pl.BlockSpec((T_ATT, hps * D_VHEAD), lambda b, h, t, qt, kt: (b * nq + kt[t], h)),
            pl.BlockSpec((hps, 2, SUB, SUB), lambda b, h, t, qt, kt: (h, 0, 0, 0)),
            vec, vec, vec, vec,
            pl.BlockSpec((1, D_VHEAD), lambda b, h, t, qt, kt: (0, 0)),
        ],
        out_specs=pl.BlockSpec((T_ATT, hps * D_VHEAD), lambda b, h, t, qt, kt: (b * nq + qt[t], h)),
        scratch_shapes=[pltpu.VMEM((hps, 2, 1, T_ATT), F32),
                        pltpu.VMEM((hps, 2, 1, T_ATT), F32),
                        pltpu.VMEM((hps, 2, D_VHEAD, T_ATT), F32),
                        pltpu.VMEM((hps, 2, T_ATT, T_ATT), F32)],
    )
    return pl.pallas_call(
        _attn_prompt_kernel,
        grid_spec=grid_spec,
        out_shape=jax.ShapeDtypeStruct((batch * seq, N_HEADS * D_VHEAD), BF16),
        compiler_params=pltpu.CompilerParams(
            dimension_semantics=("parallel", "parallel", "arbitrary"),
            vmem_limit_bytes=VMEM_LIMIT),
        name="attn_prompt",
    )(qi_tab, ki_tab, q, k, v, dtiles, *lams, gh)


ROWS_S = 128
PAGES_PER_STEP = 4
PAGE_BUFFERS = 3


def _attn_sample_kernel(pt_ref, q_ref, *refs, n_steps, t_new):
    pps = PAGES_PER_STEP
    (ck_hbm, cv_hbm, kn_ref, vn_ref, blast_ref, bnew_ref,
     lq1_ref, lk1_ref, lq2_ref, lk2_ref, gh_ref,
     o_ref, kbuf, vbuf, sem, qt_scr, m_scr, l_scr, acc_scr) = refs
    p = pl.program_id(1)
    g = pl.program_id(0) * n_steps + p
    total = pl.num_programs(0) * n_steps
    prow = PAGE * N_HEADS

    def step_copies(step):
        slot = step % PAGE_BUFFERS
        out = []
        for r in range(pps):
            page = pt_ref[step // n_steps, (step % n_steps) * pps + r]
            src = pl.ds(pl.multiple_of(page * prow, prow), prow)
            dst = pl.ds(r * prow, prow)
            out.append(pltpu.make_async_copy(ck_hbm.at[src, :], kbuf.at[slot, dst, :], sem.at[slot]))
            out.append(pltpu.make_async_copy(cv_hbm.at[src, :], vbuf.at[slot, dst, :], sem.at[slot]))
        return out

    def start_step(step):
        for c in step_copies(step):
            c.start()

    @pl.when(g == 0)
    def _():
        for first in range(PAGE_BUFFERS - 1):
            start_step(first)

    @pl.when(g + (PAGE_BUFFERS - 1) < total)
    def _():
        start_step(g + (PAGE_BUFFERS - 1))

    for c in step_copies(g):
        c.wait()
    slot = g % PAGE_BUFFERS
    rph = 2 * t_new
    tn = (((0,), (0,)), ((), ()))

    def to_rows(x):
        return jnp.transpose(jnp.broadcast_to(x, (8, ROWS_S)))[:, :1]

    @pl.when(p == 0)
    def _():
        q = q_ref[0]
        zero = jnp.zeros((t_new, D_HEAD), F32)
        blocks = []
        for h in range(N_HEADS):
            q1 = q[:, h * 2 * D_HEAD:h * 2 * D_HEAD + D_HEAD]
            q2 = q[:, h * 2 * D_HEAD + D_HEAD:(h + 1) * 2 * D_HEAD]
            blocks += [jnp.concatenate([q1, zero], axis=1), jnp.concatenate([zero, q2], axis=1)]
        qt_scr[...] = jnp.transpose(jnp.concatenate(blocks, axis=0)).astype(BF16)
        m_scr[...] = jnp.full(m_scr.shape, -jnp.inf, F32)
        l_scr[...] = jnp.zeros(l_scr.shape, F32)
        acc_scr[...] = jnp.zeros(acc_scr.shape, F32)

    sub = lax.broadcasted_iota(jnp.int32, (N_HEADS, ROWS_S), 0)
    lane = lax.broadcasted_iota(jnp.int32, (N_HEADS, ROWS_S), 1)
    head_mask = jnp.where(sub == lane // rph, 0.0, -jnp.inf).astype(F32)

    def scores(k):
        st = jnp.dot(k, qt_scr[...], preferred_element_type=F32)
        n = st.shape[0]
        return (st.reshape(n // N_HEADS, N_HEADS, ROWS_S) + head_mask[None]).reshape(n, ROWS_S)

    def attend(sts, vs):
        m_old = m_scr[...]
        m_new = m_old
        for st in sts:
            m_new = jnp.maximum(m_new, jnp.max(st, axis=0, keepdims=True))
        alpha = jnp.exp2(m_old - m_new)
        l_new = alpha * l_scr[...]
        pv = None
        for st, v in zip(sts, vs):
            pr = jnp.exp2(st - m_new)
            l_new = l_new + jnp.sum(pr, axis=0, keepdims=True)
            part = lax.dot_general(pr.astype(BF16), v, tn, preferred_element_type=F32)
            pv = part if pv is None else pv + part
        m_scr[...] = m_new
        l_scr[...] = l_new
        acc_scr[...] = to_rows(alpha) * acc_scr[...] + pv

    sts = [scores(kbuf[slot, r * prow:(r + 1) * prow, :].astype(BF16)) for r in range(pps)]
    sts[-1] = sts[-1] + jnp.where(p == n_steps - 1, blast_ref[...], 0.0)
    attend(sts, [vbuf[slot, r * prow:(r + 1) * prow, :].astype(BF16) for r in range(pps)])

    @pl.when(p == n_steps - 1)
    def _():
        attend([scores(kn_ref[...].astype(BF16)) + bnew_ref[...]], [vn_ref[...].astype(BF16)])
        lam = _lambda_full(lq1_ref, lk1_ref, lq2_ref, lk2_ref)
        a_all = acc_scr[...] / to_rows(l_scr[...])
        for h in range(N_HEADS):
            a = a_all[h * rph:(h + 1) * rph]
            o = a[:t_new] - lam * a[t_new:]
            o_ref[0, :, h * D_VHEAD:(h + 1) * D_VHEAD] = _head_norm(o, gh_ref[...])


def _attn_sample(page_table, q, cache_k, cache_v, k_new, v_new, bias_last, bias_new, lams, gh):
    n_seq, t_new, width = q.shape
    pps = PAGES_PER_STEP
    n_steps = page_table.shape[1] // pps
    assert n_steps * pps == page_table.shape[1]
    seq_blk = pl.BlockSpec((1, t_new, width), lambda s, p, pt: (s, 0, 0))
    vec = pl.BlockSpec((1, D_HEAD), lambda s, p, pt: (0, 0))
    const = lambda shape: pl.BlockSpec(shape, lambda s, p, pt: (0, 0))

    assert n_seq * n_steps >= PAGE_BUFFERS - 1
    hbm = pl.BlockSpec(memory_space=pl.ANY)
    new_blk = pl.BlockSpec((t_new * N_HEADS, D_VHEAD), lambda s, p, pt: (s, 0))

    grid_spec = pltpu.PrefetchScalarGridSpec(
        num_scalar_prefetch=1,
        grid=(n_seq, n_steps),
        in_specs=[seq_blk, hbm, hbm,
            new_blk, new_blk,
            const((PAGE * N_HEADS, ROWS_S)), const((t_new * N_HEADS, ROWS_S)),
            vec, vec, vec, vec,
            pl.BlockSpec((1, D_VHEAD), lambda s, p, pt: (0, 0)),
        ],
        out_specs=seq_blk,
        scratch_shapes=[pltpu.VMEM((PAGE_BUFFERS, pps * PAGE * N_HEADS, D_VHEAD), F32),
                        pltpu.VMEM((PAGE_BUFFERS, pps * PAGE * N_HEADS, D_VHEAD), F32),
                        pltpu.SemaphoreType.DMA((PAGE_BUFFERS,)),
                        pltpu.VMEM((2 * D_HEAD, ROWS_S), BF16),
                        pltpu.VMEM((1, ROWS_S), F32),
                        pltpu.VMEM((1, ROWS_S), F32),
                        pltpu.VMEM((ROWS_S, D_VHEAD), F32)],
    )
    return pl.pallas_call(
        functools.partial(_attn_sample_kernel, n_steps=n_steps, t_new=t_new),
        grid_spec=grid_spec,
        out_shape=jax.ShapeDtypeStruct((n_seq, t_new, width), F32),
        compiler_params=pltpu.CompilerParams(
            dimension_semantics=("arbitrary", "arbitrary"),
            vmem_limit_bytes=VMEM_LIMIT),
        name="attn_sample",
    )(page_table, q, cache_k, cache_v, k_new, v_new, bias_last, bias_new, *lams, gh)


TM_MIX = 256


def _mix_out_kernel(sv_ref, u_ref, oap_ref, oas_ref, ga_ref, gm_ref, w_ref, b_ref,
                    wo_ref, gpost_ref, h_ref, o_ref, merged_scr, *, prompt_blocks):
    is_prompt = pl.program_id(0) < prompt_blocks
    for c in range(TM_MIX // CHUNK):
        rs = slice(c * CHUNK, (c + 1) * CHUNK)
        for g in range(GROUPS):
            cs = slice(g * GDIM, (g + 1) * GDIM)
            mixed = jnp.dot(w_ref[0, g], sv_ref[rs, cs].astype(BF16),
                            preferred_element_type=F32) + b_ref[0, :, cs]
            oa = jnp.where(is_prompt, oap_ref[rs, cs].astype(F32), oas_ref[rs, cs])
            merged = (ga_ref[rs, cs].astype(F32) * oa
                      + gm_ref[rs, cs].astype(F32) * (u_ref[rs, cs].astype(F32) * mixed))
            merged_scr[rs, cs] = merged.astype(BF16)
    z = jnp.dot(merged_scr[...], wo_ref[...], preferred_element_type=F32)
    o_ref[...] = h_ref[...] + _rms_rows(z, gpost_ref[...])


def _mix_out(sv, u, oa_prompt, oa_sample, gates, w_mix, b_mix, w_o, g_post, h):
    rows = sv.shape[0]
    prompt_blocks = oa_prompt.shape[0] // TM_MIX
    blk = pl.BlockSpec((TM_MIX, D_MODEL), lambda i: (i, 0))
    kind = lambda i: jnp.where(i >= prompt_blocks, 1, 0)
    return pl.pallas_call(
        functools.partial(_mix_out_kernel, prompt_blocks=prompt_blocks),
        grid=(rows // TM_MIX,),
        in_specs=[
            blk, blk,
            pl.BlockSpec((TM_MIX, D_MODEL), lambda i: (jnp.minimum(i, prompt_blocks - 1), 0)),
            pl.BlockSpec((TM_MIX, D_MODEL), lambda i: (jnp.maximum(i - prompt_blocks, 0), 0)),
            pl.BlockSpec((TM_MIX, D_MODEL), lambda i: (i, 0)),
            pl.BlockSpec((TM_MIX, D_MODEL), lambda i: (i, 1)),
            pl.BlockSpec((1, GROUPS, CHUNK, CHUNK), lambda i: (kind(i), 0, 0, 0)),
            pl.BlockSpec((1, CHUNK, D_MODEL), lambda i: (kind(i), 0, 0)),
            pl.BlockSpec((D_MODEL, D_MODEL), lambda i: (0, 0), pipeline_mode=pl.Buffered(1)),
            pl.BlockSpec((1, D_MODEL), lambda i: (0, 0)),
            blk,
        ],
        out_specs=blk,
        out_shape=jax.ShapeDtypeStruct((rows, D_MODEL), F32),
        scratch_shapes=[pltpu.VMEM((TM_MIX, D_MODEL), BF16)],
        compiler_params=pltpu.CompilerParams(
            dimension_semantics=("parallel",),
            vmem_limit_bytes=VMEM_LIMIT),
        name="mix_out_proj",
    )(sv, u, oa_prompt, oa_sample, gates, gates, w_mix, b_mix, w_o, g_post, h)


def kernel(x_prompt, x_sample, cache_k, cache_v, page_table, rel_bias, ffn1_norm_pre, ffn1_w_gate, ffn1_w_up, ffn1_w_down, ffn1_norm_post, mix_norm_pre, w_in, lambda_q1, lambda_k1, lambda_q2, lambda_k2, attn_head_norm, gmlp_ln_g, gmlp_ln_b, gmlp_w_s, gmlp_b_s, w_o, mix_norm_post, ffn2_norm_pre, ffn2_w_gate, ffn2_w_up, ffn2_w_down, ffn2_norm_post):
    batch, seq, _ = x_prompt.shape
    n_seq, t_new, _ = x_sample.shape
    depth = cache_k.shape[0]
    assert depth == 1
    rows_p = batch * seq
    rows_s = n_seq * t_new
    rows = rows_p + rows_s
    assert t_new * 2 * N_HEADS == ROWS_S and CHUNK % t_new == 0

    row = lambda a: a.reshape(1, -1).astype(F32)
    bf = lambda a: a[0].astype(BF16)

    h1, n1 = _ffn(x_prompt.reshape(rows_p, D_MODEL), row(ffn1_norm_pre), bf(ffn1_w_gate),
                  bf(ffn1_w_up), bf(ffn1_w_down), row(ffn1_norm_post), row(mix_norm_pre),
                  row0=0, rows=rows, emit_next=True, x2=x_sample.reshape(rows_s, D_MODEL))

    w_in_b = bf(w_in)
    (q_p,) = _proj(n1, w_in_b, sec0=0, nsec=1, row0=0, rows=rows_p, mode="q_bf16")
    (q_s,) = _proj(n1, w_in_b, sec0=0, nsec=1, row0=rows_p, rows=rows_s, mode="q_f32")
    k_p, k_pb = _proj(n1, w_in_b, sec0=1, nsec=1, row0=0, rows=rows_p, mode="kv")
    k_s, _ = _proj(n1, w_in_b, sec0=1, nsec=1, row0=rows_p, rows=rows_s, mode="kv")
    v_p, v_pb = _proj(n1, w_in_b, sec0=2, nsec=1, row0=0, rows=rows_p, mode="kv")
    v_s, _ = _proj(n1, w_in_b, sec0=2, nsec=1, row0=rows_p, rows=rows_s, mode="kv")
    (u,) = _proj(n1, w_in_b, sec0=3, nsec=1, row0=0, rows=rows, mode="gelu")
    (sv,) = _proj(n1, w_in_b, sec0=4, nsec=1, row0=0, rows=rows, mode="gelu_ln",
                  extra=(row(gmlp_ln_g), row(gmlp_ln_b)))
    (gates,) = _proj(n1, w_in_b, sec0=5, nsec=2, row0=0, rows=rows, mode="sigmoid")

    tbl = _bias_by_distance(rel_bias, SUB + 1)
    g = ((tbl - rel_bias[N_BUCKETS - 1][None, :].astype(F32)) * LOG2E).T
    vec = jnp.concatenate([g[:, ::-1], jnp.zeros((N_HEADS, SUB), F32)], axis=1)
    a = jnp.tile(vec, (1, SUB))[:, :SUB * 2 * SUB].reshape(N_HEADS, SUB, 2 * SUB)
    ii = jnp.arange(SUB)
    d1 = a[:, :, :SUB]
    d0 = jnp.where(ii[:, None] >= ii[None, :], a[:, :, SUB:], -jnp.inf)
    dtiles = jnp.stack([d0, d1], axis=1)

    def for_sample(b):
        n = b.shape[-1]
        bt = jnp.transpose(b, (2, 0, 1))[:, None, :, None, :]
        return jnp.broadcast_to(bt, (n, N_HEADS, N_HEADS, 2, t_new)).reshape(n * N_HEADS, ROWS_S)

    bias_last = for_sample(d1[:, :t_new, :])
    bias_new = for_sample(d0[:, :t_new, :t_new])

    lams = (row(lambda_q1), row(lambda_k1), row(lambda_q2), row(lambda_k2))
    gh = row(attn_head_norm)

    oa_p = _attn_prompt(q_p, k_pb, v_pb, jnp.swapaxes(dtiles, 2, 3), lams, gh, batch=batch, seq=seq)
    width = N_HEADS * D_VHEAD
    oa_s = _attn_sample(page_table,
                        q_s.reshape(n_seq, t_new, width),
                        cache_k.reshape(-1, D_VHEAD),
                        cache_v.reshape(-1, D_VHEAD),
                        k_s.reshape(-1, D_VHEAD), v_s.reshape(-1, D_VHEAD),
                        bias_last, bias_new, lams, gh)

    w_s = gmlp_w_s[0].astype(F32)
    w_prompt = jnp.tril(w_s)
    w_small = jnp.tril(w_s[:, :t_new, :t_new])
    eye = jnp.eye(CHUNK // t_new, dtype=F32)
    w_sample = jnp.einsum("ab,gts->gatbs", eye, w_small).reshape(GROUPS, CHUNK, CHUNK)
    w_mix = jnp.stack([w_prompt, w_sample]).astype(BF16)
    b_s = gmlp_b_s[0].astype(F32)
    b_prompt = jnp.repeat(b_s.T, GDIM, axis=1)
    b_sample = jnp.tile(jnp.repeat(b_s[:, :t_new].T, GDIM, axis=1), (CHUNK // t_new, 1))
    b_mix = jnp.stack([b_prompt, b_sample])

    h2 = _mix_out(sv, u, oa_p, oa_s.reshape(rows_s, width), gates, w_mix, b_mix,
                  bf(w_o), row(mix_norm_post), h1)

    f2 = (row(ffn2_norm_pre), bf(ffn2_w_gate), bf(ffn2_w_up), bf(ffn2_w_down),
          row(ffn2_norm_post), row(ffn2_norm_post))
    (y_p,) = _ffn(h2, *f2, row0=0, rows=rows_p, emit_next=False)
    (y_s,) = _ffn(h2, *f2, row0=rows_p, rows=rows_s, emit_next=False)

    return (y_p.reshape(batch, seq, D_MODEL),
            y_s.reshape(n_seq, t_new, D_MODEL),
            k_p.reshape(1, batch, seq, N_HEADS, 2 * D_HEAD),
            v_p.reshape(1, batch, seq, N_HEADS, D_VHEAD),
            k_s.reshape(1, n_seq, t_new, N_HEADS, 2 * D_HEAD),
            v_s.reshape(1, n_seq, t_new, N_HEADS, D_VHEAD),
            sv[rows_p:].reshape(1, n_seq, t_new, D_MODEL))
```
